```python
import math
import jax, jax.numpy as jnp
from jax import lax
import numpy as np

D_MODEL = 1024
BATCH = 2
SEQ = 8192
DEPTH = 1

PLE_DIM = 256
HEAD_DIM = 64
DIFF_HEADS = 4
DIFF_QK = 2 * HEAD_DIM
DIFF_V = 2 * HEAD_DIM
DIFF_WIDTH = DIFF_HEADS * DIFF_V
DIL_HEADS = 8
DIL_WIDTH = DIL_HEADS * HEAD_DIM
DIL_PATTERNS = ((128, 1), (512, 4), (2048, 16))
MIX_WIDTH = DIFF_WIDTH + DIL_WIDTH
Q_BLOCK = 128
NORM_EPS = 1e-6
MASK_VALUE = -1e30

IN_SPLIT_SIZES = (
    DIFF_HEADS * DIFF_QK,
    DIFF_HEADS * DIFF_QK,
    DIFF_WIDTH,
    DIL_WIDTH,
    DIL_WIDTH,
    DIL_WIDTH,
    MIX_WIDTH,
)
IN_WIDTH = sum(IN_SPLIT_SIZES)
IN_SPLIT_IDX = tuple(int(v) for v in np.cumsum(IN_SPLIT_SIZES)[:-1])

kernel_name = "hymba_diff_dilated_alibi_ple_encoder"


def rms_norm(t, g):
    tf = t.astype(jnp.float32)
    y = tf * lax.rsqrt(jnp.mean(tf * tf, axis=-1, keepdims=True) + NORM_EPS)
    return (y * g.astype(jnp.float32)).astype(t.dtype)


def alibi_slopes(n_heads):
    return jnp.exp2(-8.0 * jnp.arange(1, n_heads + 1, dtype=jnp.float32) / n_heads)


def diff_attention(q, k, v, lam, slopes):
    B, S, H, _, Dh = q.shape
    scale = Dh ** -0.5
    nqb = S // Q_BLOCK
    pos = jnp.arange(S)
    qb = q.reshape(B, nqb, Q_BLOCK, H, 2, Dh).transpose(1, 0, 2, 3, 4, 5)
    qpos = pos.reshape(nqb, Q_BLOCK)

    def block(args):
        qblk, qp = args
        s = jnp.einsum('bqhce,bkhce->bchqk', qblk, k).astype(jnp.float32) * scale
        dist = jnp.abs(qp[:, None] - pos[None, :]).astype(jnp.float32)
        s = s - slopes[:, None, None] * dist
        a = jax.nn.softmax(s, axis=-1)
        w = a[:, 0] - lam * a[:, 1]
        return jnp.einsum('bhqk,bkhe->bqhe', w.astype(v.dtype), v)

    out = lax.map(block, (qb, qpos))
    return out.transpose(1, 0, 2, 3, 4).reshape(B, S, H, 2 * Dh)


def dilated_branch(q, k, v, slopes, window, dilation):
    B, S, H, Dh = q.shape
    r = window // (2 * dilation)
    blk = r
    unit = dilation * blk
    Lp = -(-S // unit) * unit
    n = Lp // dilation
    nb = n // blk
    scale = Dh ** -0.5

    def split(t):
        t = jnp.pad(t, ((0, 0), (0, Lp - S), (0, 0), (0, 0)))
        return t.reshape(B, nb, blk, dilation, H, Dh)

    def window3(t):
        tp = jnp.pad(t, ((0, 0), (1, 1), (0, 0), (0, 0), (0, 0), (0, 0)))
        return jnp.concatenate([tp[:, :-2], tp[:, 1:-1], tp[:, 2:]], axis=2)

    qs = split(q)
    kw = window3(split(k))
    vw = window3(split(v))

    qi = jnp.arange(nb)[:, None] * blk + jnp.arange(blk)[None, :]
    kj = (jnp.arange(nb)[:, None] - 1) * blk + jnp.arange(3 * blk)[None, :]
    rel = qi[:, :, None] - kj[:, None, :]
    kpos = kj[:, :, None] * dilation + jnp.arange(dilation)[None, None, :]
    key_ok = ((kj >= 0)[:, :, None] & (kpos < S)).transpose(0, 2, 1)
    mask = (jnp.abs(rel) <= r)[:, None, None] & key_ok[:, :, None, None, :]
    dist = (dilation * jnp.abs(rel)).astype(jnp.float32)[:, None, None]
    bias = -slopes[:, None, None] * dist

    s = jnp.einsum('bnqche,bnkche->bnchqk', qs, kw).astype(jnp.float32) * scale + bias
    s = jnp.where(mask, s, MASK_VALUE)
    m = jnp.max(s, axis=-1, keepdims=True)
    e = jnp.exp(s - m)
    l = jnp.sum(e, axis=-1, keepdims=True)
    o = jnp.einsum('bnchqk,bnkche->bnqche', e.astype(v.dtype), vw)
    inv_l = (1.0 / l[..., 0]).transpose(0, 1, 4, 2, 3)[..., None]
    o = (o.astype(jnp.float32) * inv_l).reshape(B, Lp, H, Dh)[:, :S]
    lse = (m[..., 0] + jnp.log(l[..., 0])).transpose(0, 1, 4, 2, 3).reshape(B, Lp, H)[:, :S]
    return o, lse


def dilated_attention(q, k, v, slopes):
    outs, lses = [], []
    for window, dilation in DIL_PATTERNS:
        o, lse = dilated_branch(q, k, v, slopes, window, dilation)
        outs.append(o)
        lses.append(lse)
    w = jax.nn.softmax(jnp.stack(lses, axis=0), axis=0)
    out = jnp.sum(w[..., None] * jnp.stack(outs, axis=0), axis=0)
    return out.astype(q.dtype)


def setup_inputs(seed: int = 0) -> dict:
    key = jax.random.key(seed)
    ks = jax.random.split(key, 20)
    f32 = jnp.float32

    def gain(k, n):
        return 1.0 + 0.02 * jax.random.normal(k, (DEPTH, n), f32)

    return {
        "x": jax.random.normal(ks[0], (BATCH, SEQ, D_MODEL), f32),
        "p": jax.random.normal(ks[1], (DEPTH, BATCH, SEQ, PLE_DIM), f32),
        "mix_norm_g": gain(ks[2], D_MODEL),
        "w_in": jax.random.normal(ks[3], (DEPTH, D_MODEL, IN_WIDTH), f32) * D_MODEL ** -0.5,
        "diff_q_norm_g": gain(ks[4], HEAD_DIM),
        "diff_k_norm_g": gain(ks[5], HEAD_DIM),
        "lambda_q1": 0.1 * jax.random.normal(ks[6], (DEPTH, HEAD_DIM), f32),
        "lambda_k1": 0.1 * jax.random.normal(ks[7], (DEPTH, HEAD_DIM), f32),
        "lambda_q2": 0.1 * jax.random.normal(ks[8], (DEPTH, HEAD_DIM), f32),
        "lambda_k2": 0.1 * jax.random.normal(ks[9], (DEPTH, HEAD_DIM), f32),
        "diff_sub_norm_g": gain(ks[10], DIFF_V),
        "dil_q_norm_g": gain(ks[11], HEAD_DIM),
        "dil_k_norm_g": gain(ks[12], HEAD_DIM),
        "w_out": jax.random.normal(ks[13], (DEPTH, MIX_WIDTH, D_MODEL), f32) * MIX_WIDTH ** -0.5,
        "ple_norm_g": gain(ks[14], D_MODEL),
        "w_ple_gate": jax.random.normal(ks[15], (DEPTH, D_MODEL, D_MODEL), f32) * D_MODEL ** -0.5,
        "w_ple_proj": jax.random.normal(ks[16], (DEPTH, PLE_DIM, D_MODEL), f32) * PLE_DIM ** -0.5,
    }


def reference(x, p, mix_norm_g, w_in, diff_q_norm_g, diff_k_norm_g,
              lambda_q1, lambda_k1, lambda_q2, lambda_k2, diff_sub_norm_g,
              dil_q_norm_g, dil_k_norm_g, w_out, ple_norm_g, w_ple_gate, w_ple_proj):
    B, S, _ = x.shape
    diff_slopes = alibi_slopes(DIFF_HEADS)
    dil_slopes = alibi_slopes(DIL_HEADS)
    for i in range(DEPTH):
        lam_init = 0.8 - 0.6 * math.exp(-0.3 * i)
        h = rms_norm(x, mix_norm_g[i])
        u = h @ w_in[i]
        dq, dk, dv, bq, bk, bv, z = jnp.split(u, IN_SPLIT_IDX, axis=-1)

        dq = rms_norm(dq.reshape(B, S, DIFF_HEADS, 2, HEAD_DIM), diff_q_norm_g[i])
        dk = rms_norm(dk.reshape(B, S, DIFF_HEADS, 2, HEAD_DIM), diff_k_norm_g[i])
        dv = dv.reshape(B, S, DIFF_HEADS, DIFF_V)
        lam = (jnp.exp(jnp.sum(lambda_q1[i].astype(jnp.float32) * lambda_k1[i].astype(jnp.float32)))
               - jnp.exp(jnp.sum(lambda_q2[i].astype(jnp.float32) * lambda_k2[i].astype(jnp.float32)))
               + lam_init)
        a = diff_attention(dq, dk, dv, lam, diff_slopes)
        a = rms_norm(a, diff_sub_norm_g[i]) * (1.0 - lam_init)
        a = a.reshape(B, S, DIFF_WIDTH)

        bq = rms_norm(bq.reshape(B, S, DIL_HEADS, HEAD_DIM), dil_q_norm_g[i])
        bk = rms_norm(bk.reshape(B, S, DIL_HEADS, HEAD_DIM), dil_k_norm_g[i])
        bv = bv.reshape(B, S, DIL_HEADS, HEAD_DIM)
        b = dilated_attention(bq, bk, bv, dil_slopes).reshape(B, S, DIL_WIDTH)

        y = jnp.concatenate([a, b], axis=-1) * jax.nn.silu(z)
        x = x + y @ w_out[i]

        gate = jax.nn.sigmoid(rms_norm(x, ple_norm_g[i]) @ w_ple_gate[i])
        x = x + gate * (p[i] @ w_ple_proj[i])
    return x
```

```python
import functools
import math

import jax
import jax.numpy as jnp
from jax import lax
from jax.experimental import pallas as pl
from jax.experimental.pallas import tpu as pltpu

F32 = jnp.float32
BF16 = jnp.bfloat16

HEAD_DIM = 64
DIFF_HEADS = 4
DIFF_WIDTH = DIFF_HEADS * 2 * HEAD_DIM
DIL_HEADS = 8
DIL_WIDTH = DIL_HEADS * HEAD_DIM
DIL_PATTERNS = ((128, 1), (512, 4), (2048, 16))
MIX_WIDTH = DIFF_WIDTH + DIL_WIDTH
NORM_EPS = 1e-6
MASK_VALUE = -1e30
LOG2E = math.log2(math.e)
QK_SCALE = HEAD_DIM ** -0.5

V7X_LANES = 128
PROJ_ROWS = 512
DIFF_TQ = 256
DIFF_TK = 512
DIL_ROWS = 256
DIL_RADIUS = 64
OUT_ROWS = 512
VMEM_LIMIT = 48 * 1024 * 1024


def _alibi_slopes(n_heads):
    return jnp.exp2(-8.0 * jnp.arange(1, n_heads + 1, dtype=F32) / n_heads)


def _pair_head_norm(t, gain_row):
    lane = lax.broadcasted_iota(jnp.int32, t.shape, 1)
    lo = lane < HEAD_DIM
    sq = t * t
    ms_lo = jnp.sum(jnp.where(lo, sq, 0.0), axis=-1, keepdims=True) * (1.0 / HEAD_DIM)
    ms_hi = jnp.sum(jnp.where(lo, 0.0, sq), axis=-1, keepdims=True) * (1.0 / HEAD_DIM)
    r = jnp.where(lo, lax.rsqrt(ms_lo + NORM_EPS), lax.rsqrt(ms_hi + NORM_EPS))
    return t * r * gain_row


def _proj_kernel(x_ref, g_ref, wqv_t_ref, w_ref, gq_col_ref, gk_row_ref,
                 gbq_row_ref, gbk_row_ref,
                 qt_ref, vt_ref, k_ref, bq_ref, bk_ref, bv_ref, sz_ref):
    x = x_ref[0]
    ms = jnp.mean(x * x, axis=-1, keepdims=True)
    h = (x * lax.rsqrt(ms + NORM_EPS) * g_ref[...]).astype(BF16)

    ut = lax.dot_general(wqv_t_ref[...], h, (((1,), (1,)), ((), ())),
                         preferred_element_type=F32)
    for grp in range(DIFF_WIDTH // HEAD_DIM):
        t = ut[grp * HEAD_DIM:(grp + 1) * HEAD_DIM, :]
        msq = jnp.mean(t * t, axis=0, keepdims=True)
        y = t * lax.rsqrt(msq + NORM_EPS) * gq_col_ref[...]
        qt_ref[0, grp * HEAD_DIM:(grp + 1) * HEAD_DIM, :] = y.astype(BF16)
    vt = ut[DIFF_WIDTH:, :].astype(BF16)
    for j in range(PROJ_ROWS // DIFF_TK):
        vt_ref[0, j] = vt[:, j * DIFF_TK:(j + 1) * DIFF_TK]

    u = jnp.dot(h, w_ref[...], preferred_element_type=F32)
    for pair in range(DIFF_WIDTH // V7X_LANES):
        c0 = pair * V7X_LANES
        k_ref[0, :, c0:c0 + V7X_LANES] = _pair_head_norm(
            u[:, c0:c0 + V7X_LANES], gk_row_ref[...]).astype(BF16)
    for pair in range(DIL_WIDTH // V7X_LANES):
        c0 = pair * V7X_LANES
        bq_ref[0, :, c0:c0 + V7X_LANES] = _pair_head_norm(
            u[:, 512 + c0:512 + c0 + V7X_LANES], gbq_row_ref[...]).astype(BF16)
        bk_ref[0, :, c0:c0 + V7X_LANES] = _pair_head_norm(
            u[:, 1024 + c0:1024 + c0 + V7X_LANES], gbk_row_ref[...]).astype(BF16)
    bv_ref[0] = u[:, 1536:2048].astype(BF16)
    z = u[:, 2048:]
    sz_ref[0] = z * jax.nn.sigmoid(z)


def _proj(x, g, wqv_t, w_rest, gq_col, gk_row, gbq_row, gbk_row):
    b, s, d = x.shape
    n_row = s // PROJ_ROWS
    full = lambda shape: pl.BlockSpec(shape, lambda bi, i: (0,) * len(shape))
    out_shape = (
        jax.ShapeDtypeStruct((b, DIFF_WIDTH, s), BF16),
        jax.ShapeDtypeStruct((b, s // DIFF_TK, DIFF_WIDTH, DIFF_TK), BF16),
        jax.ShapeDtypeStruct((b, s, DIFF_WIDTH), BF16),
        jax.ShapeDtypeStruct((b, s, DIL_WIDTH), BF16),
        jax.ShapeDtypeStruct((b, s, DIL_WIDTH), BF16),
        jax.ShapeDtypeStruct((b, s, DIL_WIDTH), BF16),
        jax.ShapeDtypeStruct((b, s, MIX_WIDTH), F32),
    )
    row_spec = lambda w: pl.BlockSpec((1, PROJ_ROWS, w), lambda bi, i: (bi, i, 0))
    return pl.pallas_call(
        _proj_kernel,
        grid=(b, n_row),
        in_specs=[
            pl.BlockSpec((1, PROJ_ROWS, d), lambda bi, i: (bi, i, 0)),
            full((1, d)), full(wqv_t.shape), full(w_rest.shape),
            full((HEAD_DIM, 1)), full((1, V7X_LANES)), full((1, V7X_LANES)), full((1, V7X_LANES)),
        ],
        out_specs=(
            pl.BlockSpec((1, DIFF_WIDTH, PROJ_ROWS), lambda bi, i: (bi, 0, i)),
            pl.BlockSpec((1, PROJ_ROWS // DIFF_TK, DIFF_WIDTH, DIFF_TK), lambda bi, i: (bi, i, 0, 0)),
            row_spec(DIFF_WIDTH), row_spec(DIL_WIDTH), row_spec(DIL_WIDTH), row_spec(DIL_WIDTH),
            row_spec(MIX_WIDTH),
        ),
        out_shape=out_shape,
        compiler_params=pltpu.CompilerParams(
            dimension_semantics=("arbitrary", "arbitrary"), vmem_limit_bytes=VMEM_LIMIT),
        name="proj",
    )(x, g, wqv_t, w_rest, gq_col, gk_row, gbq_row, gbk_row)


def _diff_kernel(slope_ref, lam_ref, gsub_col_ref, qt_ref, k_ref, vt_ref, o_ref,
                 acc_ref, *, lam_init, n_kv):
    head = pl.program_id(1)
    iq = pl.program_id(2)
    slope = slope_ref[head]
    qt = qt_ref[0]
    row = lax.broadcasted_iota(jnp.int32, qt.shape, 0)
    zero = jnp.zeros_like(qt)
    w_half = (jnp.where(row < HEAD_DIM, qt, zero), jnp.where(row < HEAD_DIM, zero, qt))
    rel = (lax.broadcasted_iota(jnp.int32, (DIFF_TK, DIFF_TQ), 0)
           - lax.broadcasted_iota(jnp.int32, (DIFF_TK, DIFF_TQ), 1)).astype(F32)
    q0 = (iq * DIFF_TQ).astype(F32)
    acc_ref[...] = jnp.zeros_like(acc_ref)

    def body(kv, carry):
        j0 = pl.multiple_of(kv * DIFF_TK, DIFF_TK)
        kt = k_ref[0, pl.ds(j0, DIFF_TK), :]
        vt = vt_ref[0, kv]
        bias = slope * jnp.abs(rel + (j0.astype(F32) - q0))
        new = []
        for c in range(2):
            m, l = carry[c]
            s = jnp.dot(kt, w_half[c], preferred_element_type=F32) - bias
            m_new = jnp.maximum(m, jnp.max(s, axis=0, keepdims=True))
            alpha = jnp.exp2(m - m_new)
            p = jnp.exp2(s - m_new)
            l = alpha * l + jnp.sum(p, axis=0, keepdims=True)
            acc_ref[c] = alpha * acc_ref[c] + jnp.dot(vt, p.astype(BF16),
                                                      preferred_element_type=F32)
            new.append((m_new, l))
        return tuple(new)

    init = (jnp.full((1, DIFF_TQ), MASK_VALUE, F32), jnp.zeros((1, DIFF_TQ), F32))
    (_, l0), (_, l1) = lax.fori_loop(0, n_kv, body, (init, init))

    lam_p = lam_ref[...]
    lam = (jnp.exp(jnp.sum(lam_p[0:1] * lam_p[1:2], axis=-1, keepdims=True))
           - jnp.exp(jnp.sum(lam_p[2:3] * lam_p[3:4], axis=-1, keepdims=True)) + lam_init)
    a = acc_ref[0] / l0 - lam * (acc_ref[1] / l1)
    ms = jnp.mean(a * a, axis=0, keepdims=True)
    y = a * lax.rsqrt(ms + NORM_EPS) * gsub_col_ref[...] * (1.0 - lam_init)
    o_ref[0] = y.T


def _diff_attention(slopes_l2e, lam_params, gsub_col, qt, k, vt, lam_init):
    b, s, _ = k.shape
    n_kv = s // DIFF_TK
    kern = functools.partial(_diff_kernel, lam_init=lam_init, n_kv=n_kv)
    return pl.pallas_call(
        kern,
        grid=(b, DIFF_HEADS, s // DIFF_TQ),
        in_specs=[
            pl.BlockSpec(memory_space=pltpu.SMEM),
            pl.BlockSpec((4, HEAD_DIM), lambda bi, h, i: (0, 0)),
            pl.BlockSpec((2 * HEAD_DIM, 1), lambda bi, h, i: (0, 0)),
            pl.BlockSpec((1, 2 * HEAD_DIM, DIFF_TQ), lambda bi, h, i: (bi, h, i)),
            pl.BlockSpec((1, s, 2 * HEAD_DIM), lambda bi, h, i: (bi, 0, h)),
            pl.BlockSpec((1, n_kv, 2 * HEAD_DIM, DIFF_TK), lambda bi, h, i: (bi, 0, h, 0)),
        ],
        out_specs=pl.BlockSpec((1, DIFF_TQ, 2 * HEAD_DIM), lambda bi, h, i: (bi, i, h)),
        out_shape=jax.ShapeDtypeStruct((b, s, DIFF_WIDTH), F32),
        scratch_shapes=[pltpu.VMEM((2, 2 * HEAD_DIM, DIFF_TQ), F32)],
        compiler_params=pltpu.CompilerParams(
            dimension_semantics=("arbitrary", "arbitrary", "arbitrary"),
            vmem_limit_bytes=VMEM_LIMIT),
        name="diff_attn",
    )(slopes_l2e, lam_params, gsub_col, qt, k, vt)


def _dil_kernel(q_ref, kp_ref, kc_ref, kn_ref, vp_ref, vc_ref, vn_ref, o_ref, lse_ref,
                *, dilation, n_class_rows, slopes_l2e):
    i = pl.program_id(2)
    rows = DIL_ROWS
    win = rows + 2 * DIL_RADIUS
    kw = jnp.concatenate([kp_ref[0], kc_ref[0], kn_ref[0]], axis=0)
    vw = jnp.concatenate([vp_ref[0], vc_ref[0], vn_ref[0]], axis=0)
    r_idx = lax.broadcasted_iota(jnp.int32, (rows, win), 0)
    w_idx = lax.broadcasted_iota(jnp.int32, (rows, win), 1)
    rel = jnp.abs(r_idx - (w_idx - DIL_RADIUS))
    kj = i * rows - DIL_RADIUS + w_idx
    valid = (rel <= DIL_RADIUS) & (kj >= 0) & (kj < n_class_rows)
    dist = (rel * dilation).astype(F32)
    lane = lax.broadcasted_iota(jnp.int32, (rows, V7X_LANES), 1)
    lo = lane < HEAD_DIM
    for pair in range(DIL_WIDTH // V7X_LANES):
        c0 = pair * V7X_LANES
        qp = q_ref[0, :, c0:c0 + V7X_LANES]
        kp = kw[:, c0:c0 + V7X_LANES]
        vp = vw[:, c0:c0 + V7X_LANES]
        halves = []
        for e in range(2):
            qm = jnp.where(lo if e == 0 else jnp.logical_not(lo), qp, jnp.zeros_like(qp))
            s = lax.dot_general(qm, kp, (((1,), (1,)), ((), ())),
                                preferred_element_type=F32)
            s = jnp.where(valid, s - slopes_l2e[2 * pair + e] * dist, MASK_VALUE)
            m = jnp.max(s, axis=-1, keepdims=True)
            p = jnp.exp2(s - m)
            l = jnp.sum(p, axis=-1, keepdims=True)
            o = jnp.dot(p.astype(BF16), vp, preferred_element_type=F32)
            halves.append((o / l, m + jnp.log2(l)))
        o_ref[0, :, c0:c0 + V7X_LANES] = jnp.where(lo, halves[0][0], halves[1][0])
        lse_ref[0, :, c0:c0 + V7X_LANES] = jnp.where(lo, halves[0][1], halves[1][1])


def _dilated_pattern(bq, bk, bv, dilation, slopes_l2e):
    b, s, w = bq.shape
    n = s // dilation
    view = lambda t: t.reshape(b, n, dilation * w)
    halo_per_block = DIL_ROWS // DIL_RADIUS
    n_halo = n // DIL_RADIUS
    cur = pl.BlockSpec((1, DIL_ROWS, w), lambda bi, c, i: (bi, i, c))
    prev = pl.BlockSpec((1, DIL_RADIUS, w),
                        lambda bi, c, i: (bi, jnp.maximum(i * halo_per_block - 1, 0), c))
    nxt = pl.BlockSpec((1, DIL_RADIUS, w),
                       lambda bi, c, i: (bi, jnp.minimum((i + 1) * halo_per_block, n_halo - 1), c))
    kern = functools.partial(_dil_kernel, dilation=dilation, n_class_rows=n,
                             slopes_l2e=slopes_l2e)
    o, lse = pl.pallas_call(
        kern,
        grid=(b, dilation, n // DIL_ROWS),
        in_specs=[cur, prev, cur, nxt, prev, cur, nxt],
        out_specs=(cur, cur),
        out_shape=(jax.ShapeDtypeStruct((b, n, dilation * w), F32),
                   jax.ShapeDtypeStruct((b, n, dilation * w), F32)),
        compiler_params=pltpu.CompilerParams(
            dimension_semantics=("arbitrary", "arbitrary", "arbitrary"),
            vmem_limit_bytes=VMEM_LIMIT),
        name=f"dil_attn_d{dilation}",
    )(view(bq), view(bk), view(bk), view(bk), view(bv), view(bv), view(bv))
    return o.reshape(b, s, w), lse.reshape(b, s, w)


def _out_kernel(x_ref, p_ref, a_ref, sz_ref, o1_ref, o2_ref, o3_ref, l1_ref, l2_ref, l3_ref,
                wout_ref, g2_ref, wpg_ref, wpp_ref, y_ref):
    l1, l2, l3 = l1_ref[0], l2_ref[0], l3_ref[0]
    m = jnp.maximum(jnp.maximum(l1, l2), l3)
    w1, w2, w3 = jnp.exp2(l1 - m), jnp.exp2(l2 - m), jnp.exp2(l3 - m)
    bmix = (w1 * o1_ref[0] + w2 * o2_ref[0] + w3 * o3_ref[0]) / (w1 + w2 + w3)
    sz = sz_ref[0]
    ya = (a_ref[0] * sz[:, :DIFF_WIDTH]).astype(BF16)
    yb = (bmix * sz[:, DIFF_WIDTH:]).astype(BF16)
    x1 = (x_ref[0]
          + jnp.dot(ya, wout_ref[:DIFF_WIDTH, :], preferred_element_type=F32)
          + jnp.dot(yb, wout_ref[DIFF_WIDTH:, :], preferred_element_type=F32))
    ms = jnp.mean(x1 * x1, axis=-1, keepdims=True)
    h2 = (x1 * lax.rsqrt(ms + NORM_EPS) * g2_ref[...]).astype(BF16)
    gate = jax.nn.sigmoid(jnp.dot(h2, wpg_ref[...], preferred_element_type=F32))
    pp = jnp.dot(p_ref[0].astype(BF16), wpp_ref[...], preferred_element_type=F32)
    y_ref[0] = x1 + gate * pp


def _out(x, p, a, sz, outs, lses, wout, g2, wpg, wpp):
    b, s, d = x.shape
    full = lambda shape: pl.BlockSpec(shape, lambda bi, i: (0,) * len(shape))
    row_spec = lambda w: pl.BlockSpec((1, OUT_ROWS, w), lambda bi, i: (bi, i, 0))
    return pl.pallas_call(
        _out_kernel,
        grid=(b, s // OUT_ROWS),
        in_specs=[row_spec(d), row_spec(p.shape[-1]), row_spec(DIFF_WIDTH), row_spec(MIX_WIDTH)]
        + [row_spec(DIL_WIDTH)] * 6
        + [full(wout.shape), full((1, d)), full(wpg.shape), full(wpp.shape)],
        out_specs=row_spec(d),
        out_shape=jax.ShapeDtypeStruct((b, s, d), F32),
        compiler_params=pltpu.CompilerParams(
            dimension_semantics=("arbitrary", "arbitrary"), vmem_limit_bytes=VMEM_LIMIT),
        name="out_proj",
    )(x, p, a, sz, *outs, *lses, wout, g2, wpg, wpp)


def kernel(x, p, mix_norm_g, w_in, diff_q_norm_g, diff_k_norm_g, lambda_q1, lambda_k1,
           lambda_q2, lambda_k2, diff_sub_norm_g, dil_q_norm_g, dil_k_norm_g, w_out,
           ple_norm_g, w_ple_gate, w_ple_proj):
    depth = w_in.shape[0]
    diff_slopes = _alibi_slopes(DIFF_HEADS) * LOG2E
    dil_slopes = tuple(float(2.0 ** (-8.0 * (i + 1) / DIL_HEADS)) * LOG2E for i in range(DIL_HEADS))
    fold = QK_SCALE * LOG2E
    for i in range(depth):
        lam_init = 0.8 - 0.6 * math.exp(-0.3 * i)
        w = w_in[i]
        wqv_t = jnp.concatenate([w[:, :512], w[:, 1024:1536]], axis=1).T.astype(BF16)
        w_rest = jnp.concatenate([w[:, 512:1024], w[:, 1536:]], axis=1).astype(BF16)
        row2 = lambda g: jnp.tile(g, 2)[None, :]
        qt, vt, k, bq, bk, bv, sz = _proj(
            x, mix_norm_g[i][None, :], wqv_t, w_rest,
            (diff_q_norm_g[i] * fold)[:, None], row2(diff_k_norm_g[i]),
            row2(dil_q_norm_g[i] * fold), row2(dil_k_norm_g[i]))
        lam_params = jnp.stack([lambda_q1[i], lambda_k1[i], lambda_q2[i], lambda_k2[i]])
        a = _diff_attention(diff_slopes, lam_params, diff_sub_norm_g[i][:, None],
                            qt, k, vt, lam_init)
        outs, lses = [], []
        for _, dilation in DIL_PATTERNS:
            o, lse = _dilated_pattern(bq, bk, bv, dilation, dil_slopes)
            outs.append(o)
            lses.append(lse)
        x = _out(x, p[i], a, sz, outs, lses, w_out[i].astype(BF16), ple_norm_g[i][None, :],
                 w_ple_gate[i].astype(BF16), w_ple_proj[i].astype(BF16))
    return x
```

```python
import functools
import math

import jax
import jax.numpy as jnp
from jax import lax
from jax.experimental import pallas as pl
from jax.experimental.pallas import tpu as pltpu

F32 = jnp.float32
BF16 = jnp.bfloat16

HEAD_DIM = 64
DIFF_HEADS = 4
DIFF_V = 2 * HEAD_DIM
DIFF_WIDTH = DIFF_HEADS * DIFF_V
DIL_HEADS = 8
DIL_WIDTH = DIL_HEADS * HEAD_DIM
DIL_PATTERNS = ((128, 1), (512, 4), (2048, 16))
MIX_WIDTH = DIFF_WIDTH + DIL_WIDTH
NORM_EPS = 1e-6
MASK_VALUE = -1e30
LOG2E = math.log2(math.e)
QK_SCALE = HEAD_DIM ** -0.5

V7X_LANES = 128
PROJ_ROWS = 512
DIFF_TQ = 256
DIFF_TK = 512
DIFF_FAST_TILES = 2
VT_ROWS = DIFF_V + 16
DIFF_FAST_MAX_SCORE = 40.0
DIL_ROWS = 256
DIL_RADIUS = 64
OUT_ROWS = 512
VMEM_LIMIT = 48 * 1024 * 1024


def _alibi_slopes(n_heads):
    return jnp.exp2(-8.0 * jnp.arange(1, n_heads + 1, dtype=F32) / n_heads)


def _pair_head_norm(t, gain_row):
    lane = lax.broadcasted_iota(jnp.int32, t.shape, 1)
    lo = lane < HEAD_DIM
    sq = t * t
    ms_lo = jnp.sum(jnp.where(lo, sq, 0.0), axis=-1, keepdims=True) * (1.0 / HEAD_DIM)
    ms_hi = jnp.sum(jnp.where(lo, 0.0, sq), axis=-1, keepdims=True) * (1.0 / HEAD_DIM)
    r = jnp.where(lo, lax.rsqrt(ms_lo + NORM_EPS), lax.rsqrt(ms_hi + NORM_EPS))
    return t * r * gain_row


def _proj_kernel(x_ref, g_ref, wqv_t_ref, w_ref, gq_col_ref, gk_row_ref,
                 gbq_row_ref, gbk_row_ref,
                 qt_ref, vt_ref, k_ref, bq_ref, bk_ref, bv_ref, sz_ref):
    x = x_ref[0]
    ms = jnp.mean(x * x, axis=-1, keepdims=True)
    h = (x * lax.rsqrt(ms + NORM_EPS) * g_ref[...]).astype(BF16)

    ut = lax.dot_general(wqv_t_ref[...], h, (((1,), (1,)), ((), ())),
                         preferred_element_type=F32)
    for grp in range(DIFF_WIDTH // HEAD_DIM):
        t = ut[grp * HEAD_DIM:(grp + 1) * HEAD_DIM, :]
        msq = jnp.mean(t * t, axis=0, keepdims=True)
        y = t * lax.rsqrt(msq + NORM_EPS) * gq_col_ref[...]
        qt_ref[0, grp * HEAD_DIM:(grp + 1) * HEAD_DIM, :] = y.astype(BF16)
    vt = ut[DIFF_WIDTH:, :].astype(BF16)
    ones = jnp.ones((VT_ROWS - DIFF_V, DIFF_TK), BF16)
    for j in range(PROJ_ROWS // DIFF_TK):
        for hd in range(DIFF_HEADS):
            vt_ref[0, j, hd * VT_ROWS:hd * VT_ROWS + DIFF_V, :] = (
                vt[hd * DIFF_V:(hd + 1) * DIFF_V, j * DIFF_TK:(j + 1) * DIFF_TK])
            vt_ref[0, j, hd * VT_ROWS + DIFF_V:(hd + 1) * VT_ROWS, :] = ones

    u = jnp.dot(h, w_ref[...], preferred_element_type=F32)
    for pair in range(DIFF_WIDTH // V7X_LANES):
        c0 = pair * V7X_LANES
        k_ref[0, :, c0:c0 + V7X_LANES] = _pair_head_norm(
            u[:, c0:c0 + V7X_LANES], gk_row_ref[...]).astype(BF16)
    for pair in range(DIL_WIDTH // V7X_LANES):
        c0 = pair * V7X_LANES
        bq_ref[0, :, c0:c0 + V7X_LANES] = _pair_head_norm(
            u[:, 512 + c0:512 + c0 + V7X_LANES], gbq_row_ref[...]).astype(BF16)
        bk_ref[0, :, c0:c0 + V7X_LANES] = _pair_head_norm(
            u[:, 1024 + c0:1024 + c0 + V7X_LANES], gbk_row_ref[...]).astype(BF16)
    bv_ref[0] = u[:, 1536:2048].astype(BF16)
    z = u[:, 2048:]
    sz_ref[0] = z * jax.nn.sigmoid(z)


def _proj(x, g, wqv_t, w_rest, gq_col, gk_row, gbq_row, gbk_row):
    b, s, d = x.shape
    n_row = s // PROJ_ROWS
    full = lambda shape: pl.BlockSpec(shape, lambda bi, i: (0,) * len(shape))
    out_shape = (
        jax.ShapeDtypeStruct((b, DIFF_WIDTH, s), BF16),
        jax.ShapeDtypeStruct((b, s // DIFF_TK, DIFF_HEADS * VT_ROWS, DIFF_TK), BF16),
        jax.ShapeDtypeStruct((b, s, DIFF_WIDTH), BF16),
        jax.ShapeDtypeStruct((b, s, DIL_WIDTH), BF16),
        jax.ShapeDtypeStruct((b, s, DIL_WIDTH), BF16),
        jax.ShapeDtypeStruct((b, s, DIL_WIDTH), BF16),
        jax.ShapeDtypeStruct((b, s, MIX_WIDTH), F32),
    )
    row_spec = lambda w: pl.BlockSpec((1, PROJ_ROWS, w), lambda bi, i: (bi, i, 0))
    return pl.pallas_call(
        _proj_kernel,
        grid=(b, n_row),
        in_specs=[
            pl.BlockSpec((1, PROJ_ROWS, d), lambda bi, i: (bi, i, 0)),
            full((1, d)), full(wqv_t.shape), full(w_rest.shape),
            full((HEAD_DIM, 1)), full((1, V7X_LANES)), full((1, V7X_LANES)), full((1, V7X_LANES)),
        ],
        out_specs=(
            pl.BlockSpec((1, DIFF_WIDTH, PROJ_ROWS), lambda bi, i: (bi, 0, i)),
            pl.BlockSpec((1, PROJ_ROWS // DIFF_TK, DIFF_HEADS * VT_ROWS, DIFF_TK),
                         lambda bi, i: (bi, i, 0, 0)),
            row_spec(DIFF_WIDTH), row_spec(DIL_WIDTH), row_spec(DIL_WIDTH), row_spec(DIL_WIDTH),
            row_spec(MIX_WIDTH),
        ),
        out_shape=out_shape,
        compiler_params=pltpu.CompilerParams(
            dimension_semantics=("arbitrary", "arbitrary"), vmem_limit_bytes=VMEM_LIMIT),
        name="proj",
    )(x, g, wqv_t, w_rest, gq_col, gk_row, gbq_row, gbk_row)


def _diff_kernel(bound_ref, slope_ref, lam_ref, gsub_col_ref, qt_ref, k_ref, vt_ref,
                 o_ref, acc_ref, sa_ref, sb_ref, *, lam_init, n_kv):
    head = pl.program_id(1)
    iq = pl.program_id(2)
    slope = slope_ref[head]
    qt = qt_ref[0]
    row = lax.broadcasted_iota(jnp.int32, qt.shape, 0)
    zero = jnp.zeros_like(qt)
    w_half = (jnp.where(row < HEAD_DIM, qt, zero), jnp.where(row < HEAD_DIM, zero, qt))
    i0 = iq * DIFF_TQ
    acc_ref[...] = jnp.zeros_like(acc_ref)
    shift_free = bound_ref[0] <= DIFF_FAST_MAX_SCORE

    def tile_bias(j0, tk):
        rel = (lax.broadcasted_iota(jnp.int32, (tk, DIFF_TQ), 0)
               - lax.broadcasted_iota(jnp.int32, (tk, DIFF_TQ), 1) + (j0 - i0))
        return slope * jnp.abs(rel).astype(F32)

    @pl.when(shift_free)
    def _():
        tk = DIFF_FAST_TILES * DIFF_TK
        rel = (lax.broadcasted_iota(jnp.int32, (tk, DIFF_TQ), 0)
               - lax.broadcasted_iota(jnp.int32, (tk, DIFF_TQ), 1)).astype(F32)

        n_t = n_kv // DIFF_FAST_TILES

        def scores(t, s_ref):
            j0 = pl.multiple_of(t * tk, tk)
            kt = k_ref[0, pl.ds(j0, tk), :]
            bias = slope * jnp.abs(rel + (j0 - i0).astype(F32))
            for c in range(2):
                s_ref[c] = jnp.dot(kt, w_half[c], preferred_element_type=F32) - bias

        def accumulate(t, s_ref):
            vt = jnp.concatenate(
                [vt_ref[0, t * DIFF_FAST_TILES + ch] for ch in range(DIFF_FAST_TILES)], axis=1)
            for c in range(2):
                acc_ref[c] += jnp.dot(vt, jnp.exp2(s_ref[c]).astype(BF16),
                                      preferred_element_type=F32)

        scores(0, sa_ref)

        def body(i, carry):
            scores(2 * i + 1, sb_ref)
            accumulate(2 * i, sa_ref)
            scores(2 * i + 2, sa_ref)
            accumulate(2 * i + 1, sb_ref)
            return carry

        lax.fori_loop(0, n_t // 2 - 1, body, 0)
        scores(n_t - 1, sb_ref)
        accumulate(n_t - 2, sa_ref)
        accumulate(n_t - 1, sb_ref)

    @pl.when(jnp.logical_not(shift_free))
    def _():
        def body(kv, carry):
            j0 = pl.multiple_of(kv * DIFF_TK, DIFF_TK)
            kt = k_ref[0, pl.ds(j0, DIFF_TK), :]
            vt = vt_ref[0, kv]
            bias = tile_bias(j0, DIFF_TK)
            new = []
            for c in range(2):
                s = jnp.dot(kt, w_half[c], preferred_element_type=F32) - bias
                m_new = jnp.maximum(carry[c], jnp.max(s, axis=0, keepdims=True))
                alpha = jnp.exp2(carry[c] - m_new)
                p = jnp.exp2(s - m_new)
                acc_ref[c] = alpha * acc_ref[c] + jnp.dot(vt, p.astype(BF16),
                                                          preferred_element_type=F32)
                new.append(m_new)
            return tuple(new)

        init = jnp.full((1, DIFF_TQ), MASK_VALUE, F32)
        lax.fori_loop(0, n_kv, body, (init, init))

    lam_p = lam_ref[...]
    lam = (jnp.exp(jnp.sum(lam_p[0:1] * lam_p[1:2], axis=-1, keepdims=True))
           - jnp.exp(jnp.sum(lam_p[2:3] * lam_p[3:4], axis=-1, keepdims=True)) + lam_init)
    acc0, acc1 = acc_ref[0], acc_ref[1]
    a = (acc0[:DIFF_V] / acc0[DIFF_V:DIFF_V + 1]
         - lam * (acc1[:DIFF_V] / acc1[DIFF_V:DIFF_V + 1]))
    ms = jnp.mean(a * a, axis=0, keepdims=True)
    y = a * lax.rsqrt(ms + NORM_EPS) * gsub_col_ref[...] * (1.0 - lam_init)
    o_ref[0] = y.T


def _diff_attention(score_bound, slopes_l2e, lam_params, gsub_col, qt, k, vt, lam_init):
    b, s, _ = k.shape
    n_kv = s // DIFF_TK
    kern = functools.partial(_diff_kernel, lam_init=lam_init, n_kv=n_kv)
    return pl.pallas_call(
        kern,
        grid=(b, DIFF_HEADS, s // DIFF_TQ),
        in_specs=[
            pl.BlockSpec(memory_space=pltpu.SMEM),
            pl.BlockSpec(memory_space=pltpu.SMEM),
            pl.BlockSpec((4, HEAD_DIM), lambda bi, h, i: (0, 0)),
            pl.BlockSpec((DIFF_V, 1), lambda bi, h, i: (0, 0)),
            pl.BlockSpec((1, DIFF_V, DIFF_TQ), lambda bi, h, i: (bi, h, i)),
            pl.BlockSpec((1, s, DIFF_V), lambda bi, h, i: (bi, 0, h)),
            pl.BlockSpec((1, n_kv, VT_ROWS, DIFF_TK), lambda bi, h, i: (bi, 0, h, 0)),
        ],
        out_specs=pl.BlockSpec((1, DIFF_TQ, DIFF_V), lambda bi, h, i: (bi, i, h)),
        out_shape=jax.ShapeDtypeStruct((b, s, DIFF_WIDTH), F32),
        scratch_shapes=[pltpu.VMEM((2, VT_ROWS, DIFF_TQ), F32),
                        pltpu.VMEM((2, DIFF_FAST_TILES * DIFF_TK, DIFF_TQ), F32),
                        pltpu.VMEM((2, DIFF_FAST_TILES * DIFF_TK, DIFF_TQ), F32)],
        compiler_params=pltpu.CompilerParams(
            dimension_semantics=("arbitrary", "arbitrary", "arbitrary"),
            vmem_limit_bytes=VMEM_LIMIT),
        name="diff_attn",
    )(score_bound, slopes_l2e, lam_params, gsub_col, qt, k, vt)


def _dil_kernel(q_ref, kp_ref, kc_ref, kn_ref, vp_ref, vc_ref, vn_ref, o_ref, lse_ref,
                *, dilation, n_class_rows, slopes_l2e):
    i = pl.program_id(2)
    rows = DIL_ROWS
    win = rows + 2 * DIL_RADIUS
    kw = jnp.concatenate([kp_ref[0], kc_ref[0], kn_ref[0]], axis=0)
    vw = jnp.concatenate([vp_ref[0], vc_ref[0], vn_ref[0]], axis=0)
    r_idx = lax.broadcasted_iota(jnp.int32, (rows, win), 0)
    w_idx = lax.broadcasted_iota(jnp.int32, (rows, win), 1)
    rel = jnp.abs(r_idx - (w_idx - DIL_RADIUS))
    kj = i * rows - DIL_RADIUS + w_idx
    valid = (rel <= DIL_RADIUS) & (kj >= 0) & (kj < n_class_rows)
    dist = (rel * dilation).astype(F32)
    lane = lax.broadcasted_iota(jnp.int32, (rows, V7X_LANES), 1)
    lo = lane < HEAD_DIM
    for pair in range(DIL_WIDTH // V7X_LANES):
        c0 = pair * V7X_LANES
        qp = q_ref[0, :, c0:c0 + V7X_LANES]
        kp = kw[:, c0:c0 + V7X_LANES]
        vp = vw[:, c0:c0 + V7X_LANES]
        halves = []
        for e in range(2):
            qm = jnp.where(lo if e == 0 else jnp.logical_not(lo), qp, jnp.zeros_like(qp))
            s = lax.dot_general(qm, kp, (((1,), (1,)), ((), ())),
                                preferred_element_type=F32)
            s = jnp.where(valid, s - slopes_l2e[2 * pair + e] * dist, MASK_VALUE)
            m = jnp.max(s, axis=-1, keepdims=True)
            p = jnp.exp2(s - m)
            l = jnp.sum(p, axis=-1, keepdims=True)
            o = jnp.dot(p.astype(BF16), vp, preferred_element_type=F32)
            halves.append((o / l, m + jnp.log2(l)))
        o_ref[0, :, c0:c0 + V7X_LANES] = jnp.where(lo, halves[0][0], halves[1][0])
        lse_ref[0, :, c0:c0 + V7X_LANES] = jnp.where(lo, halves[0][1], halves[1][1])


def _dilated_pattern(bq, bk, bv, dilation, slopes_l2e):
    b, s, w = bq.shape
    n = s // dilation
    view = lambda t: t.reshape(b, n, dilation * w)
    halo_per_block = DIL_ROWS // DIL_RADIUS
    n_halo = n // DIL_RADIUS
    cur = pl.BlockSpec((1, DIL_ROWS, w), lambda bi, c, i: (bi, i, c))
    prev = pl.BlockSpec((1, DIL_RADIUS, w),
                        lambda bi, c, i: (bi, jnp.maximum(i * halo_per_block - 1, 0), c))
    nxt = pl.BlockSpec((1, DIL_RADIUS, w),
                       lambda bi, c, i: (bi, jnp.minimum((i + 1) * halo_per_block, n_halo - 1), c))
    kern = functools.partial(_dil_kernel, dilation=dilation, n_class_rows=n,
                             slopes_l2e=slopes_l2e)
    o, lse = pl.pallas_call(
        kern,
        grid=(b, dilation, n // DIL_ROWS),
        in_specs=[cur, prev, cur, nxt, prev, cur, nxt],
        out_specs=(cur, cur),
        out_shape=(jax.ShapeDtypeStruct((b, n, dilation * w), F32),
                   jax.ShapeDtypeStruct((b, n, dilation * w), F32)),
        compiler_params=pltpu.CompilerParams(
            dimension_semantics=("arbitrary", "arbitrary", "arbitrary"),
            vmem_limit_bytes=VMEM_LIMIT),
        name=f"dil_attn_d{dilation}",
    )(view(bq), view(bk), view(bk), view(bk), view(bv), view(bv), view(bv))
    return o.reshape(b, s, w), lse.reshape(b, s, w)


def _out_kernel(x_ref, p_ref, a_ref, sz_ref, o1_ref, o2_ref, o3_ref, l1_ref, l2_ref, l3_ref,
                wout_ref, g2_ref, wpg_ref, wpp_ref, y_ref):
    l1, l2, l3 = l1_ref[0], l2_ref[0], l3_ref[0]
    m = jnp.maximum(jnp.maximum(l1, l2), l3)
    w1, w2, w3 = jnp.exp2(l1 - m), jnp.exp2(l2 - m), jnp.exp2(l3 - m)
    bmix = (w1 * o1_ref[0] + w2 * o2_ref[0] + w3 * o3_ref[0]) / (w1 + w2 + w3)
    sz = sz_ref[0]
    ya = (a_ref[0] * sz[:, :DIFF_WIDTH]).astype(BF16)
    yb = (bmix * sz[:, DIFF_WIDTH:]).astype(BF16)
    x1 = (x_ref[0]
          + jnp.dot(ya, wout_ref[:DIFF_WIDTH, :], preferred_element_type=F32)
          + jnp.dot(yb, wout_ref[DIFF_WIDTH:, :], preferred_element_type=F32))
    ms = jnp.mean(x1 * x1, axis=-1, keepdims=True)
    h2 = (x1 * lax.rsqrt(ms + NORM_EPS) * g2_ref[...]).astype(BF16)
    gate = jax.nn.sigmoid(jnp.dot(h2, wpg_ref[...], preferred_element_type=F32))
    pp = jnp.dot(p_ref[0].astype(BF16), wpp_ref[...], preferred_element_type=F32)
    y_ref[0] = x1 + gate * pp


def _out(x, p, a, sz, outs, lses, wout, g2, wpg, wpp):
    b, s, d = x.shape
    full = lambda shape: pl.BlockSpec(shape, lambda bi, i: (0,) * len(shape))
    row_spec = lambda w: pl.BlockSpec((1, OUT_ROWS, w), lambda bi, i: (bi, i, 0))
    return pl.pallas_call(
        _out_kernel,
        grid=(b, s // OUT_ROWS),
        in_specs=[row_spec(d), row_spec(p.shape[-1]), row_spec(DIFF_WIDTH), row_spec(MIX_WIDTH)]
        + [row_spec(DIL_WIDTH)] * 6
        + [full(wout.shape), full((1, d)), full(wpg.shape), full(wpp.shape)],
        out_specs=row_spec(d),
        out_shape=jax.ShapeDtypeStruct((b, s, d), F32),
        compiler_params=pltpu.CompilerParams(
            dimension_semantics=("arbitrary", "arbitrary"), vmem_limit_bytes=VMEM_LIMIT),
        name="out_proj",
    )(x, p, a, sz, *outs, *lses, wout, g2, wpg, wpp)


def kernel(x, p, mix_norm_g, w_in, diff_q_norm_g, diff_k_norm_g, lambda_q1, lambda_k1,
           lambda_q2, lambda_k2, diff_sub_norm_g, dil_q_norm_g, dil_k_norm_g, w_out,
           ple_norm_g, w_ple_gate, w_ple_proj):
    depth = w_in.shape[0]
    diff_slopes = _alibi_slopes(DIFF_HEADS) * LOG2E
    dil_slopes = tuple(float(2.0 ** (-8.0 * (i + 1) / DIL_HEADS)) * LOG2E for i in range(DIL_HEADS))
    fold = QK_SCALE * LOG2E
    for i in range(depth):
        lam_init = 0.8 - 0.6 * math.exp(-0.3 * i)
        w = w_in[i]
        wqv_t = jnp.concatenate([w[:, :512], w[:, 1024:1536]], axis=1).T.astype(BF16)
        w_rest = jnp.concatenate([w[:, 512:1024], w[:, 1536:]], axis=1).astype(BF16)
        row2 = lambda g: jnp.tile(g, 2)[None, :]
        qt, vt, k, bq, bk, bv, sz = _proj(
            x, mix_norm_g[i][None, :], wqv_t, w_rest,
            (diff_q_norm_g[i] * fold)[:, None], row2(diff_k_norm_g[i]),
            row2(dil_q_norm_g[i] * fold), row2(dil_k_norm_g[i]))
        lam_params = jnp.stack([lambda_q1[i], lambda_k1[i], lambda_q2[i], lambda_k2[i]])
        score_bound = (1.01 * HEAD_DIM * fold * jnp.max(jnp.abs(diff_q_norm_g[i]))
                       * jnp.max(jnp.abs(diff_k_norm_g[i]))).reshape(1)
        a = _diff_attention(score_bound, diff_slopes, lam_params, diff_sub_norm_g[i][:, None],
                            qt, k, vt, lam_init)
        outs, lses = [], []
        for _, dilation in DIL_PATTERNS:
            o, lse = _dilated_pattern(bq, bk, bv, dilation, dil_slopes)
            outs.append(o)
            lses.append(lse)
        x = _out(x, p[i], a, sz, outs, lses, w_out[i].astype(BF16), ple_norm_g[i][None, :],
                 w_ple_gate[i].astype(BF16), w_ple_proj[i].astype(BF16))
    return x
```

```python
import functools
import math

import jax
import jax.numpy as jnp
from jax import lax
from jax.experimental import pallas as pl
from jax.experimental.pallas import tpu as pltpu

F32 = jnp.float32
BF16 = jnp.bfloat16

HEAD_DIM = 64
DIFF_HEADS = 4
DIFF_V = 2 * HEAD_DIM
DIFF_WIDTH = DIFF_HEADS * DIFF_V
DIL_HEADS = 8
DIL_WIDTH = DIL_HEADS * HEAD_DIM
DIL_PATTERNS = ((128, 1), (512, 4), (2048, 16))
MIX_WIDTH = DIFF_WIDTH + DIL_WIDTH
NORM_EPS = 1e-6
MASK_VALUE = -1e30
LOG2E = math.log2(math.e)
QK_SCALE = HEAD_DIM ** -0.5

V7X_LANES = 128
PROJ_ROWS = 512
DIFF_TQ = 256
DIFF_TK = 512
DIFF_FAST_TILES = 2
VT_ROWS = DIFF_V + 16
FAST_MAX_SCORE = 40.0
DIL_ROWS = 256
DIL_SB = 1024
DIL_Q = 128
DIL_VMEM_LIMIT = 56 * 1024 * 1024
DIL_RADIUS = 64
OUT_ROWS = 512
VMEM_LIMIT = 48 * 1024 * 1024


def _alibi_slopes(n_heads):
    return jnp.exp2(-8.0 * jnp.arange(1, n_heads + 1, dtype=F32) / n_heads)


def _pair_head_norm(t, gain_row):
    lane = lax.broadcasted_iota(jnp.int32, t.shape, 1)
    lo = lane < HEAD_DIM
    sq = t * t
    ms_lo = jnp.sum(jnp.where(lo, sq, 0.0), axis=-1, keepdims=True) * (1.0 / HEAD_DIM)
    ms_hi = jnp.sum(jnp.where(lo, 0.0, sq), axis=-1, keepdims=True) * (1.0 / HEAD_DIM)
    r = jnp.where(lo, lax.rsqrt(ms_lo + NORM_EPS), lax.rsqrt(ms_hi + NORM_EPS))
    return t * r * gain_row


def _proj_kernel(x_ref, g_ref, wqv_t_ref, w_ref, gq_col_ref, gk_row_ref,
                 gbq_row_ref, gbk_row_ref,
                 qt_ref, vt_ref, k_ref, bq_ref, bk_ref, bv_ref,
                 bq4_ref, bk4_ref, bv4_ref, bq16_ref, bk16_ref, bv16_ref, sz_ref, ub_ref):
    x = x_ref[0]
    ms = jnp.mean(x * x, axis=-1, keepdims=True)
    h = (x * lax.rsqrt(ms + NORM_EPS) * g_ref[...]).astype(BF16)

    ut = lax.dot_general(wqv_t_ref[...], h, (((1,), (1,)), ((), ())),
                         preferred_element_type=F32)
    for grp in range(DIFF_WIDTH // HEAD_DIM):
        t = ut[grp * HEAD_DIM:(grp + 1) * HEAD_DIM, :]
        msq = jnp.mean(t * t, axis=0, keepdims=True)
        y = t * lax.rsqrt(msq + NORM_EPS) * gq_col_ref[...]
        qt_ref[0, grp * HEAD_DIM:(grp + 1) * HEAD_DIM, :] = y.astype(BF16)
    vt = ut[DIFF_WIDTH:, :].astype(BF16)
    ones = jnp.ones((VT_ROWS - DIFF_V, DIFF_TK), BF16)
    for j in range(PROJ_ROWS // DIFF_TK):
        for hd in range(DIFF_HEADS):
            vt_ref[0, j, hd * VT_ROWS:hd * VT_ROWS + DIFF_V, :] = (
                vt[hd * DIFF_V:(hd + 1) * DIFF_V, j * DIFF_TK:(j + 1) * DIFF_TK])
            vt_ref[0, j, hd * VT_ROWS + DIFF_V:(hd + 1) * VT_ROWS, :] = ones

    u = jnp.dot(h, w_ref[...], preferred_element_type=F32)
    for pair in range(DIFF_WIDTH // V7X_LANES):
        c0 = pair * V7X_LANES
        k_ref[0, :, c0:c0 + V7X_LANES] = _pair_head_norm(
            u[:, c0:c0 + V7X_LANES], gk_row_ref[...]).astype(BF16)
    n_col = DIL_WIDTH // V7X_LANES
    for pair in range(n_col):
        c0 = pair * V7X_LANES
        ub_ref[pair] = _pair_head_norm(u[:, 512 + c0:512 + c0 + V7X_LANES], gbq_row_ref[...])
        ub_ref[n_col + pair] = _pair_head_norm(u[:, 1024 + c0:1024 + c0 + V7X_LANES],
                                               gbk_row_ref[...])
        ub_ref[2 * n_col + pair] = u[:, 1536 + c0:1536 + c0 + V7X_LANES]
    by_dilation = {1: (bq_ref, bk_ref, bv_ref), 4: (bq4_ref, bk4_ref, bv4_ref),
                   16: (bq16_ref, bk16_ref, bv16_ref)}
    for dilation, refs in by_dilation.items():
        n = PROJ_ROWS // dilation
        for cls in range(dilation):
            for tns in range(3):
                for pair in range(n_col):
                    rows = ub_ref[tns * n_col + pair, pl.ds(cls, n, stride=dilation), :]
                    c0 = cls * DIL_WIDTH + pair * V7X_LANES
                    refs[tns][0, :, c0:c0 + V7X_LANES] = rows.astype(BF16)
    z = u[:, 2048:]
    sz_ref[0] = z * jax.nn.sigmoid(z)


def _proj(x, g, wqv_t, w_rest, gq_col, gk_row, gbq_row, gbk_row):
    b, s, d = x.shape
    n_row = s // PROJ_ROWS
    full = lambda shape: pl.BlockSpec(shape, lambda bi, i: (0,) * len(shape))
    out_shape = (
        jax.ShapeDtypeStruct((b, DIFF_WIDTH, s), BF16),
        jax.ShapeDtypeStruct((b, s // DIFF_TK, DIFF_HEADS * VT_ROWS, DIFF_TK), BF16),
        jax.ShapeDtypeStruct((b, s, DIFF_WIDTH), BF16),
    ) + tuple(
        jax.ShapeDtypeStruct((b, s // dil, dil * DIL_WIDTH), BF16)
        for _, dil in DIL_PATTERNS for _ in range(3)
    ) + (jax.ShapeDtypeStruct((b, s, MIX_WIDTH), F32),)
    row_spec = lambda w: pl.BlockSpec((1, PROJ_ROWS, w), lambda bi, i: (bi, i, 0))
    view_spec = lambda dil: pl.BlockSpec((1, PROJ_ROWS // dil, dil * DIL_WIDTH),
                                         lambda bi, i: (bi, i, 0))
    return pl.pallas_call(
        _proj_kernel,
        grid=(b, n_row),
        in_specs=[
            pl.BlockSpec((1, PROJ_ROWS, d), lambda bi, i: (bi, i, 0)),
            full((1, d)), full(wqv_t.shape), full(w_rest.shape),
            full((HEAD_DIM, 1)), full((1, V7X_LANES)), full((1, V7X_LANES)), full((1, V7X_LANES)),
        ],
        out_specs=(
            pl.BlockSpec((1, DIFF_WIDTH, PROJ_ROWS), lambda bi, i: (bi, 0, i)),
            pl.BlockSpec((1, PROJ_ROWS // DIFF_TK, DIFF_HEADS * VT_ROWS, DIFF_TK),
                         lambda bi, i: (bi, i, 0, 0)),
            row_spec(DIFF_WIDTH),
        ) + tuple(view_spec(dil) for _, dil in DIL_PATTERNS for _ in range(3)) + (
            row_spec(MIX_WIDTH),
        ),
        out_shape=out_shape,
        scratch_shapes=[pltpu.VMEM((3 * DIL_WIDTH // V7X_LANES, PROJ_ROWS, V7X_LANES), F32)],
        compiler_params=pltpu.CompilerParams(
            dimension_semantics=("arbitrary", "arbitrary"), vmem_limit_bytes=VMEM_LIMIT),
        name="proj",
    )(x, g, wqv_t, w_rest, gq_col, gk_row, gbq_row, gbk_row)


def _diff_kernel(bound_ref, slope_ref, lam_ref, gsub_col_ref, qt_ref, k_ref, vt_ref,
                 o_ref, acc_ref, sa_ref, sb_ref, *, lam_init, n_kv):
    head = pl.program_id(1)
    iq = pl.program_id(2)
    slope = slope_ref[head]
    qt = qt_ref[0]
    row = lax.broadcasted_iota(jnp.int32, qt.shape, 0)
    zero = jnp.zeros_like(qt)
    w_half = (jnp.where(row < HEAD_DIM, qt, zero), jnp.where(row < HEAD_DIM, zero, qt))
    i0 = iq * DIFF_TQ
    acc_ref[...] = jnp.zeros_like(acc_ref)
    shift_free = bound_ref[0] <= FAST_MAX_SCORE

    def tile_bias(j0, tk):
        rel = (lax.broadcasted_iota(jnp.int32, (tk, DIFF_TQ), 0)
               - lax.broadcasted_iota(jnp.int32, (tk, DIFF_TQ), 1) + (j0 - i0))
        return slope * jnp.abs(rel).astype(F32)

    @pl.when(shift_free)
    def _():
        tk = DIFF_FAST_TILES * DIFF_TK
        rel = (lax.broadcasted_iota(jnp.int32, (tk, DIFF_TQ), 0)
               - lax.broadcasted_iota(jnp.int32, (tk, DIFF_TQ), 1)).astype(F32)

        n_t = n_kv // DIFF_FAST_TILES

        def scores(t, s_ref):
            j0 = pl.multiple_of(t * tk, tk)
            kt = k_ref[0, pl.ds(j0, tk), :]
            bias = slope * jnp.abs(rel + (j0 - i0).astype(F32))
            for c in range(2):
                s_ref[c] = jnp.dot(kt, w_half[c], preferred_element_type=F32) - bias

        def accumulate(t, s_ref):
            vt = jnp.concatenate(
                [vt_ref[0, t * DIFF_FAST_TILES + ch] for ch in range(DIFF_FAST_TILES)], axis=1)
            for c in range(2):
                acc_ref[c] += jnp.dot(vt, jnp.exp2(s_ref[c]).astype(BF16),
                                      preferred_element_type=F32)

        scores(0, sa_ref)

        def body(i, carry):
            scores(2 * i + 1, sb_ref)
            accumulate(2 * i, sa_ref)
            scores(2 * i + 2, sa_ref)
            accumulate(2 * i + 1, sb_ref)
            return carry

        lax.fori_loop(0, n_t // 2 - 1, body, 0)
        scores(n_t - 1, sb_ref)
        accumulate(n_t - 2, sa_ref)
        accumulate(n_t - 1, sb_ref)

    @pl.when(jnp.logical_not(shift_free))
    def _():
        def body(kv, carry):
            j0 = pl.multiple_of(kv * DIFF_TK, DIFF_TK)
            kt = k_ref[0, pl.ds(j0, DIFF_TK), :]
            vt = vt_ref[0, kv]
            bias = tile_bias(j0, DIFF_TK)
            new = []
            for c in range(2):
                s = jnp.dot(kt, w_half[c], preferred_element_type=F32) - bias
                m_new = jnp.maximum(carry[c], jnp.max(s, axis=0, keepdims=True))
                alpha = jnp.exp2(carry[c] - m_new)
                p = jnp.exp2(s - m_new)
                acc_ref[c] = alpha * acc_ref[c] + jnp.dot(vt, p.astype(BF16),
                                                          preferred_element_type=F32)
                new.append(m_new)
            return tuple(new)

        init = jnp.full((1, DIFF_TQ), MASK_VALUE, F32)
        lax.fori_loop(0, n_kv, body, (init, init))

    lam_p = lam_ref[...]
    lam = (jnp.exp(jnp.sum(lam_p[0:1] * lam_p[1:2], axis=-1, keepdims=True))
           - jnp.exp(jnp.sum(lam_p[2:3] * lam_p[3:4], axis=-1, keepdims=True)) + lam_init)
    acc0, acc1 = acc_ref[0], acc_ref[1]
    a = (acc0[:DIFF_V] / acc0[DIFF_V:DIFF_V + 1]
         - lam * (acc1[:DIFF_V] / acc1[DIFF_V:DIFF_V + 1]))
    ms = jnp.mean(a * a, axis=0, keepdims=True)
    y = a * lax.rsqrt(ms + NORM_EPS) * gsub_col_ref[...] * (1.0 - lam_init)
    o_ref[0] = y.T


def _diff_attention(score_bound, slopes_l2e, lam_params, gsub_col, qt, k, vt, lam_init):
    b, s, _ = k.shape
    n_kv = s // DIFF_TK
    kern = functools.partial(_diff_kernel, lam_init=lam_init, n_kv=n_kv)
    return pl.pallas_call(
        kern,
        grid=(b, DIFF_HEADS, s // DIFF_TQ),
        in_specs=[
            pl.BlockSpec(memory_space=pltpu.SMEM),
            pl.BlockSpec(memory_space=pltpu.SMEM),
            pl.BlockSpec((4, HEAD_DIM), lambda bi, h, i: (0, 0)),
            pl.BlockSpec((DIFF_V, 1), lambda bi, h, i: (0, 0)),
            pl.BlockSpec((1, DIFF_V, DIFF_TQ), lambda bi, h, i: (bi, h, i)),
            pl.BlockSpec((1, s, DIFF_V), lambda bi, h, i: (bi, 0, h)),
            pl.BlockSpec((1, n_kv, VT_ROWS, DIFF_TK), lambda bi, h, i: (bi, 0, h, 0)),
        ],
        out_specs=pl.BlockSpec((1, DIFF_TQ, DIFF_V), lambda bi, h, i: (bi, i, h)),
        out_shape=jax.ShapeDtypeStruct((b, s, DIFF_WIDTH), F32),
        scratch_shapes=[pltpu.VMEM((2, VT_ROWS, DIFF_TQ), F32),
                        pltpu.VMEM((2, DIFF_FAST_TILES * DIFF_TK, DIFF_TQ), F32),
                        pltpu.VMEM((2, DIFF_FAST_TILES * DIFF_TK, DIFF_TQ), F32)],
        compiler_params=pltpu.CompilerParams(
            dimension_semantics=("arbitrary", "arbitrary", "arbitrary"),
            vmem_limit_bytes=VMEM_LIMIT),
        name="diff_attn",
    )(score_bound, slopes_l2e, lam_params, gsub_col, qt, k, vt)


def _dil_kernel(q_ref, kp_ref, kc_ref, kn_ref, vp_ref, vc_ref, vn_ref, o_ref, lse_ref,
                *, dilation, n_class_rows, slopes_l2e):
    i = pl.program_id(2)
    rows = DIL_ROWS
    win = rows + 2 * DIL_RADIUS
    kw = jnp.concatenate([kp_ref[0], kc_ref[0], kn_ref[0]], axis=0)
    vw = jnp.concatenate([vp_ref[0], vc_ref[0], vn_ref[0]], axis=0)
    r_idx = lax.broadcasted_iota(jnp.int32, (rows, win), 0)
    w_idx = lax.broadcasted_iota(jnp.int32, (rows, win), 1)
    rel = jnp.abs(r_idx - (w_idx - DIL_RADIUS))
    kj = i * rows - DIL_RADIUS + w_idx
    valid = (rel <= DIL_RADIUS) & (kj >= 0) & (kj < n_class_rows)
    dist = (rel * dilation).astype(F32)
    lane = lax.broadcasted_iota(jnp.int32, (rows, V7X_LANES), 1)
    lo = lane < HEAD_DIM
    for pair in range(DIL_WIDTH // V7X_LANES):
        c0 = pair * V7X_LANES
        qp = q_ref[0, :, c0:c0 + V7X_LANES]
        kp = kw[:, c0:c0 + V7X_LANES]
        vp = vw[:, c0:c0 + V7X_LANES]
        halves = []
        for e in range(2):
            qm = jnp.where(lo if e == 0 else jnp.logical_not(lo), qp, jnp.zeros_like(qp))
            s = lax.dot_general(qm, kp, (((1,), (1,)), ((), ())),
                                preferred_element_type=F32)
            s = jnp.where(valid, s - slopes_l2e[2 * pair + e] * dist, MASK_VALUE)
            m = jnp.max(s, axis=-1, keepdims=True)
            p = jnp.exp2(s - m)
            l = jnp.sum(p, axis=-1, keepdims=True)
            o = jnp.dot(p.astype(BF16), vp, preferred_element_type=F32)
            halves.append((o / l, m + jnp.log2(l)))
        o_ref[0, :, c0:c0 + V7X_LANES] = jnp.where(lo, halves[0][0], halves[1][0])
        lse_ref[0, :, c0:c0 + V7X_LANES] = jnp.where(lo, halves[0][1], halves[1][1])


def _dilated_pattern(bq, bk, bv, dilation, slopes_l2e):
    b, n, _ = bq.shape
    w = DIL_WIDTH
    s = n * dilation
    halo_per_block = DIL_ROWS // DIL_RADIUS
    n_halo = n // DIL_RADIUS
    cur = pl.BlockSpec((1, DIL_ROWS, w), lambda bi, c, i: (bi, i, c))
    prev = pl.BlockSpec((1, DIL_RADIUS, w),
                        lambda bi, c, i: (bi, jnp.maximum(i * halo_per_block - 1, 0), c))
    nxt = pl.BlockSpec((1, DIL_RADIUS, w),
                       lambda bi, c, i: (bi, jnp.minimum((i + 1) * halo_per_block, n_halo - 1), c))
    kern = functools.partial(_dil_kernel, dilation=dilation, n_class_rows=n,
                             slopes_l2e=slopes_l2e)
    o, lse = pl.pallas_call(
        kern,
        grid=(b, dilation, n // DIL_ROWS),
        in_specs=[cur, prev, cur, nxt, prev, cur, nxt],
        out_specs=(cur, cur),
        out_shape=(jax.ShapeDtypeStruct((b, n, dilation * w), F32),
                   jax.ShapeDtypeStruct((b, n, dilation * w), F32)),
        compiler_params=pltpu.CompilerParams(
            dimension_semantics=("arbitrary", "arbitrary", "arbitrary"),
            vmem_limit_bytes=VMEM_LIMIT),
        name=f"dil_attn_d{dilation}",
    )(bq, bk, bk, bk, bv, bv, bv)
    return o.reshape(b, s, w), lse.reshape(b, s, w)


def _dil_merge_kernel(o1_ref, o2_ref, o3_ref, l1_ref, l2_ref, l3_ref, b_ref):
    l1, l2, l3 = l1_ref[0], l2_ref[0], l3_ref[0]
    m = jnp.maximum(jnp.maximum(l1, l2), l3)
    w1, w2, w3 = jnp.exp2(l1 - m), jnp.exp2(l2 - m), jnp.exp2(l3 - m)
    b_ref[0] = (w1 * o1_ref[0] + w2 * o2_ref[0] + w3 * o3_ref[0]) / (w1 + w2 + w3)


def _dilated_by_pattern(views, slopes_l2e):
    outs, lses = [], []
    for (_, dilation), (bq, bk, bv) in zip(DIL_PATTERNS, views):
        o, lse = _dilated_pattern(bq, bk, bv, dilation, slopes_l2e)
        outs.append(o)
        lses.append(lse)
    b, s, w = outs[0].shape
    spec = pl.BlockSpec((1, OUT_ROWS, w), lambda bi, i: (bi, i, 0))
    return pl.pallas_call(
        _dil_merge_kernel,
        grid=(b, s // OUT_ROWS),
        in_specs=[spec] * 6,
        out_specs=spec,
        out_shape=jax.ShapeDtypeStruct((b, s, w), F32),
        compiler_params=pltpu.CompilerParams(
            dimension_semantics=("arbitrary", "arbitrary"), vmem_limit_bytes=VMEM_LIMIT),
        name="dil_merge",
    )(*outs, *lses)


def _dil_fused_kernel(*refs, slopes_l2e):
    q_refs = refs[0:3]
    k_refs = [refs[3 + 3 * n:6 + 3 * n] for n in range(3)]
    v_refs = [refs[12 + 3 * n:15 + 3 * n] for n in range(3)]
    o_ref, tab_ref, kw_ref, vw_ref, acc_ref, l_ref = refs[21:]
    bi, i = pl.program_id(0), pl.program_id(1)
    rows, win, rad = DIL_Q, DIL_Q + 2 * DIL_RADIUS, DIL_RADIUS
    n_col = DIL_WIDTH // V7X_LANES

    @pl.when(jnp.logical_and(bi == 0, i == 0))
    def _():
        rel = jnp.abs(lax.broadcasted_iota(jnp.int32, (rows, win), 0)
                      - lax.broadcasted_iota(jnp.int32, (rows, win), 1) + rad)
        for n, (_, dilation) in enumerate(DIL_PATTERNS):
            dist = (rel * dilation).astype(F32)
            for hd in range(DIL_HEADS):
                tab_ref[n, hd // 2, (hd % 2) * rows:(hd % 2 + 1) * rows, :] = jnp.where(
                    rel <= rad, -slopes_l2e[hd] * dist, MASK_VALUE)

    col = lax.broadcasted_iota(jnp.int32, (1, win), 1)
    no_prev = jnp.where(jnp.logical_and(col < rad, i == 0), MASK_VALUE, 0.0)
    ones = jnp.ones((win, V7X_LANES), BF16)

    def no_next(start):
        return jnp.where(jnp.logical_and(col >= start, i == pl.num_programs(1) - 1),
                         MASK_VALUE, 0.0)

    def head_pair(qp, kwin, vwin, bias):
        r = qp.shape[0]
        lo = lax.broadcasted_iota(jnp.int32, (r, V7X_LANES), 1) < HEAD_DIM
        zq = jnp.zeros_like(qp)
        lhs = jnp.concatenate([jnp.where(lo, qp, zq), jnp.where(lo, zq, qp)], axis=0)
        s = lax.dot_general(lhs, kwin, (((1,), (1,)), ((), ())),
                            preferred_element_type=F32) + bias
        o = jnp.dot(jnp.exp2(s).astype(BF16), jnp.concatenate([vwin, ones], axis=1),
                    preferred_element_type=F32)
        return (jnp.where(lo, o[:r, :V7X_LANES], o[r:, :V7X_LANES]),
                jnp.where(lo, o[:r, V7X_LANES:], o[r:, V7X_LANES:]))

    def fill_window(n, cls, n_cur):
        cs = slice(cls * DIL_WIDTH, (cls + 1) * DIL_WIDTH)
        for w_ref, (p_ref, c_ref, n_ref) in ((kw_ref, k_refs[n]), (vw_ref, v_refs[n])):
            w_ref[0:rad, :] = p_ref[0, :, cs]
            w_ref[rad:rad + n_cur, :] = c_ref[0, :, cs]
            w_ref[rad + n_cur:2 * rad + n_cur, :] = n_ref[0, :, cs]

    n_sub = DIL_SB // rows
    fill_window(0, 0, DIL_SB)

    def body(j, carry):
        r0 = pl.multiple_of(j * rows, rows)
        edge = jnp.where(j == 0, no_prev, 0.0) + jnp.where(j == n_sub - 1, no_next(win - rad), 0.0)
        for g in range(n_col):
            cs = slice(g * V7X_LANES, (g + 1) * V7X_LANES)
            acc, l = head_pair(q_refs[0][0, pl.ds(r0, rows), cs], kw_ref[pl.ds(r0, win), cs],
                               vw_ref[pl.ds(r0, win), cs], tab_ref[0, g] + edge)
            acc_ref[g, pl.ds(r0, rows), :] = acc
            l_ref[g, pl.ds(r0, rows), :] = l
        return carry

    lax.fori_loop(0, n_sub, body, 0)

    for n, (_, dilation) in list(enumerate(DIL_PATTERNS))[1:]:
        n_cur = DIL_SB // dilation
        r = min(rows, n_cur)
        n_sub = n_cur // r
        for cls in range(dilation):
            fill_window(n, cls, n_cur)
            for j in range(n_sub):
                edge = ((no_prev if j == 0 else 0.0)
                        + (no_next(r + rad) if j == n_sub - 1 else 0.0))
                for g in range(n_col):
                    cs = slice(g * V7X_LANES, (g + 1) * V7X_LANES)
                    qs = slice(cls * DIL_WIDTH + g * V7X_LANES, cls * DIL_WIDTH + (g + 1) * V7X_LANES)
                    bias = jnp.concatenate([tab_ref[n, g, 0:r, :], tab_ref[n, g, rows:rows + r, :]],
                                           axis=0) + edge
                    acc, l = head_pair(q_refs[n][0, j * r:(j + 1) * r, qs],
                                       kw_ref[j * r:j * r + win, cs], vw_ref[j * r:j * r + win, cs],
                                       bias)
                    where = pl.ds(j * r * dilation + cls, r, stride=dilation)
                    acc_ref[g, where, :] += acc
                    l_ref[g, where, :] += l

    for g in range(n_col):
        o_ref[0, :, g * V7X_LANES:(g + 1) * V7X_LANES] = acc_ref[g] / l_ref[g]


def _dilated_fused(views, slopes_l2e):
    b = views[0][0].shape[0]
    s = views[0][0].shape[1]
    in_specs, args = [], []
    for (_, dilation), (bq, _, _) in zip(DIL_PATTERNS, views):
        in_specs.append(pl.BlockSpec((1, DIL_SB // dilation, dilation * DIL_WIDTH),
                                     lambda bi, i: (bi, i, 0)))
        args.append(bq)
    for tns in (1, 2):
        for (_, dilation), view in zip(DIL_PATTERNS, views):
            n_cur = DIL_SB // dilation
            per_block = n_cur // DIL_RADIUS
            n_halo = s // dilation // DIL_RADIUS
            in_specs += [
                pl.BlockSpec((1, DIL_RADIUS, dilation * DIL_WIDTH),
                             lambda bi, i, pb=per_block: (bi, jnp.maximum(i * pb - 1, 0), 0)),
                pl.BlockSpec((1, n_cur, dilation * DIL_WIDTH), lambda bi, i: (bi, i, 0)),
                pl.BlockSpec((1, DIL_RADIUS, dilation * DIL_WIDTH),
                             lambda bi, i, pb=per_block, nh=n_halo:
                             (bi, jnp.minimum((i + 1) * pb, nh - 1), 0)),
            ]
            args += [view[tns]] * 3
    n_col = DIL_WIDTH // V7X_LANES
    return pl.pallas_call(
        functools.partial(_dil_fused_kernel, slopes_l2e=slopes_l2e),
        grid=(b, s // DIL_SB),
        in_specs=in_specs,
        out_specs=pl.BlockSpec((1, DIL_SB, DIL_WIDTH), lambda bi, i: (bi, i, 0)),
        out_shape=jax.ShapeDtypeStruct((b, s, DIL_WIDTH), F32),
        scratch_shapes=[
            pltpu.VMEM((len(DIL_PATTERNS), n_col, 2 * DIL_Q, DIL_Q + 2 * DIL_RADIUS), F32),
            pltpu.VMEM((DIL_SB + 2 * DIL_RADIUS, DIL_WIDTH), BF16),
            pltpu.VMEM((DIL_SB + 2 * DIL_RADIUS, DIL_WIDTH), BF16),
            pltpu.VMEM((n_col, DIL_SB, V7X_LANES), F32),
            pltpu.VMEM((n_col, DIL_SB, V7X_LANES), F32),
        ],
        compiler_params=pltpu.CompilerParams(
            dimension_semantics=("arbitrary", "arbitrary"), vmem_limit_bytes=DIL_VMEM_LIMIT),
        name="dil_fused",
    )(*args)


def _out_kernel(x_ref, p_ref, a_ref, sz_ref, b_ref, wout_ref, g2_ref, wpg_ref, wpp_ref, y_ref):
    sz = sz_ref[0]
    ya = (a_ref[0] * sz[:, :DIFF_WIDTH]).astype(BF16)
    yb = (b_ref[0] * sz[:, DIFF_WIDTH:]).astype(BF16)
    x1 = (x_ref[0]
          + jnp.dot(ya, wout_ref[:DIFF_WIDTH, :], preferred_element_type=F32)
          + jnp.dot(yb, wout_ref[DIFF_WIDTH:, :], preferred_element_type=F32))
    ms = jnp.mean(x1 * x1, axis=-1, keepdims=True)
    h2 = (x1 * lax.rsqrt(ms + NORM_EPS) * g2_ref[...]).astype(BF16)
    gate = jax.nn.sigmoid(jnp.dot(h2, wpg_ref[...], preferred_element_type=F32))
    pp = jnp.dot(p_ref[0].astype(BF16), wpp_ref[...], preferred_element_type=F32)
    y_ref[0] = x1 + gate * pp


def _out(x, p, a, sz, bmix, wout, g2, wpg, wpp):
    b, s, d = x.shape
    full = lambda shape: pl.BlockSpec(shape, lambda bi, i: (0,) * len(shape))
    row_spec = lambda w: pl.BlockSpec((1, OUT_ROWS, w), lambda bi, i: (bi, i, 0))
    return pl.pallas_call(
        _out_kernel,
        grid=(b, s // OUT_ROWS),
        in_specs=[row_spec(d), row_spec(p.shape[-1]), row_spec(DIFF_WIDTH), row_spec(MIX_WIDTH),
                  row_spec(DIL_WIDTH),
                  full(wout.shape), full((1, d)), full(wpg.shape), full(wpp.shape)],
        out_specs=row_spec(d),
        out_shape=jax.ShapeDtypeStruct((b, s, d), F32),
        compiler_params=pltpu.CompilerParams(
            dimension_semantics=("arbitrary", "arbitrary"), vmem_limit_bytes=VMEM_LIMIT),
        name="out_proj",
    )(x, p, a, sz, bmix, wout, g2, wpg, wpp)


def kernel(x, p, mix_norm_g, w_in, diff_q_norm_g, diff_k_norm_g, lambda_q1, lambda_k1,
           lambda_q2, lambda_k2, diff_sub_norm_g, dil_q_norm_g, dil_k_norm_g, w_out,
           ple_norm_g, w_ple_gate, w_ple_proj):
    depth = w_in.shape[0]
    diff_slopes = _alibi_slopes(DIFF_HEADS) * LOG2E
    dil_slopes = tuple(float(2.0 ** (-8.0 * (i + 1) / DIL_HEADS)) * LOG2E for i in range(DIL_HEADS))
    fold = QK_SCALE * LOG2E
    for i in range(depth):
        lam_init = 0.8 - 0.6 * math.exp(-0.3 * i)
        w = w_in[i]
        wqv_t = jnp.concatenate([w[:, :512], w[:, 1024:1536]], axis=1).T.astype(BF16)
        w_rest = jnp.concatenate([w[:, 512:1024], w[:, 1536:]], axis=1).astype(BF16)
        row2 = lambda g: jnp.tile(g, 2)[None, :]
        qt, vt, k, *dil_qkv, sz = _proj(
            x, mix_norm_g[i][None, :], wqv_t, w_rest,
            (diff_q_norm_g[i] * fold)[:, None], row2(diff_k_norm_g[i]),
            row2(dil_q_norm_g[i] * fold), row2(dil_k_norm_g[i]))
        views = tuple(tuple(dil_qkv[3 * n:3 * n + 3]) for n in range(len(DIL_PATTERNS)))
        lam_params = jnp.stack([lambda_q1[i], lambda_k1[i], lambda_q2[i], lambda_k2[i]])

        def score_bound(gq, gk):
            return 1.01 * HEAD_DIM * fold * jnp.max(jnp.abs(gq)) * jnp.max(jnp.abs(gk))

        a = _diff_attention(score_bound(diff_q_norm_g[i], diff_k_norm_g[i]).reshape(1), diff_slopes,
                            lam_params, diff_sub_norm_g[i][:, None], qt, k, vt, lam_init)
        bmix = lax.cond(score_bound(dil_q_norm_g[i], dil_k_norm_g[i]) <= FAST_MAX_SCORE,
                        lambda v: _dilated_fused(v, dil_slopes),
                        lambda v: _dilated_by_pattern(v, dil_slopes), views)
        x = _out(x, p[i], a, sz, bmix, w_out[i].astype(BF16), ple_norm_g[i][None, :],
                 w_ple_gate[i].astype(BF16), w_ple_proj[i].astype(BF16))
    return x
```

```python
import functools
import math

import jax
import jax.numpy as jnp
from jax import lax
from jax.experimental import pallas as pl
from jax.experimental.pallas import tpu as pltpu

F32 = jnp.float32
BF16 = jnp.bfloat16

HEAD_DIM = 64
DIFF_HEADS = 4
DIFF_V = 2 * HEAD_DIM
DIFF_WIDTH = DIFF_HEADS * DIFF_V
DIL_HEADS = 8
DIL_WIDTH = DIL_HEADS * HEAD_DIM
DIL_PATTERNS = ((128, 1), (512, 4), (2048, 16))
MIX_WIDTH = DIFF_WIDTH + DIL_WIDTH
NORM_EPS = 1e-6
MASK_VALUE = -1e30
LOG2E = math.log2(math.e)
QK_SCALE = HEAD_DIM ** -0.5

V7X_LANES = 128
PROJ_ROWS = 512
DIFF_TQ = 256
DIFF_TK = 512
DIFF_FAST_TILES = 2
VT_ROWS = DIFF_V + 16
FAST_MAX_SCORE = 40.0
DIL_ROWS = 256
DIL_SB = 1024
DIL_Q = 128
DIL_VMEM_LIMIT = 56 * 1024 * 1024
DIL_RADIUS = 64
OUT_ROWS = 512
VMEM_LIMIT = 48 * 1024 * 1024


def _alibi_slopes(n_heads):
    return jnp.exp2(-8.0 * jnp.arange(1, n_heads + 1, dtype=F32) / n_heads)


def _pair_head_norm(t, gain_row):
    lane = lax.broadcasted_iota(jnp.int32, t.shape, 1)
    lo = lane < HEAD_DIM
    sq = t * t
    ms_lo = jnp.sum(jnp.where(lo, sq, 0.0), axis=-1, keepdims=True) * (1.0 / HEAD_DIM)
    ms_hi = jnp.sum(jnp.where(lo, 0.0, sq), axis=-1, keepdims=True) * (1.0 / HEAD_DIM)
    r = jnp.where(lo, lax.rsqrt(ms_lo + NORM_EPS), lax.rsqrt(ms_hi + NORM_EPS))
    return t * r * gain_row


def _proj_kernel(x_ref, g_ref, wqv_t_ref, w_ref, gq_col_ref, gk_row_ref,
                 gbq_row_ref, gbk_row_ref,
                 qt_ref, vt_ref, k_ref, bq_ref, bk_ref, bv_ref,
                 bq4_ref, bk4_ref, bv4_ref, bq16_ref, bk16_ref, bv16_ref, sz_ref, ub_ref):
    x = x_ref[0]
    ms = jnp.mean(x * x, axis=-1, keepdims=True)
    h = (x * lax.rsqrt(ms + NORM_EPS) * g_ref[...]).astype(BF16)

    ut = lax.dot_general(wqv_t_ref[...], h, (((1,), (1,)), ((), ())),
                         preferred_element_type=F32)
    for grp in range(DIFF_WIDTH // HEAD_DIM):
        t = ut[grp * HEAD_DIM:(grp + 1) * HEAD_DIM, :]
        msq = jnp.mean(t * t, axis=0, keepdims=True)
        y = t * lax.rsqrt(msq + NORM_EPS) * gq_col_ref[...]
        qt_ref[0, grp * HEAD_DIM:(grp + 1) * HEAD_DIM, :] = y.astype(BF16)
    vt = ut[DIFF_WIDTH:, :].astype(BF16)
    ones = jnp.ones((VT_ROWS - DIFF_V, DIFF_TK), BF16)
    for j in range(PROJ_ROWS // DIFF_TK):
        for hd in range(DIFF_HEADS):
            vt_ref[0, j, hd * VT_ROWS:hd * VT_ROWS + DIFF_V, :] = (
                vt[hd * DIFF_V:(hd + 1) * DIFF_V, j * DIFF_TK:(j + 1) * DIFF_TK])
            vt_ref[0, j, hd * VT_ROWS + DIFF_V:(hd + 1) * VT_ROWS, :] = ones

    u = jnp.dot(h, w_ref[...], preferred_element_type=F32)
    for pair in range(DIFF_WIDTH // V7X_LANES):
        c0 = pair * V7X_LANES
        k_ref[0, :, c0:c0 + V7X_LANES] = _pair_head_norm(
            u[:, c0:c0 + V7X_LANES], gk_row_ref[...]).astype(BF16)
    n_col = DIL_WIDTH // V7X_LANES
    for pair in range(n_col):
        c0 = pair * V7X_LANES
        ub_ref[pair] = _pair_head_norm(u[:, 512 + c0:512 + c0 + V7X_LANES], gbq_row_ref[...])
        ub_ref[n_col + pair] = _pair_head_norm(u[:, 1024 + c0:1024 + c0 + V7X_LANES],
                                               gbk_row_ref[...])
        ub_ref[2 * n_col + pair] = u[:, 1536 + c0:1536 + c0 + V7X_LANES]
    by_dilation = {1: (bq_ref, bk_ref, bv_ref), 4: (bq4_ref, bk4_ref, bv4_ref),
                   16: (bq16_ref, bk16_ref, bv16_ref)}
    for dilation, refs in by_dilation.items():
        n = PROJ_ROWS // dilation
        for cls in range(dilation):
            for tns in range(3):
                for pair in range(n_col):
                    rows = ub_ref[tns * n_col + pair, pl.ds(cls, n, stride=dilation), :]
                    c0 = cls * DIL_WIDTH + pair * V7X_LANES
                    refs[tns][0, :, c0:c0 + V7X_LANES] = rows.astype(BF16)
    z = u[:, 2048:]
    sz_ref[0] = z * jax.nn.sigmoid(z)


def _proj(x, g, wqv_t, w_rest, gq_col, gk_row, gbq_row, gbk_row):
    b, s, d = x.shape
    n_row = s // PROJ_ROWS
    full = lambda shape: pl.BlockSpec(shape, lambda bi, i: (0,) * len(shape))
    out_shape = (
        jax.ShapeDtypeStruct((b, DIFF_WIDTH, s), BF16),
        jax.ShapeDtypeStruct((b, s // DIFF_TK, DIFF_HEADS * VT_ROWS, DIFF_TK), BF16),
        jax.ShapeDtypeStruct((b, s, DIFF_WIDTH), BF16),
    ) + tuple(
        jax.ShapeDtypeStruct((b, s // dil, dil * DIL_WIDTH), BF16)
        for _, dil in DIL_PATTERNS for _ in range(3)
    ) + (jax.ShapeDtypeStruct((b, s, MIX_WIDTH), F32),)
    row_spec = lambda w: pl.BlockSpec((1, PROJ_ROWS, w), lambda bi, i: (bi, i, 0))
    view_spec = lambda dil: pl.BlockSpec((1, PROJ_ROWS // dil, dil * DIL_WIDTH),
                                         lambda bi, i: (bi, i, 0))
    return pl.pallas_call(
        _proj_kernel,
        grid=(b, n_row),
        in_specs=[
            pl.BlockSpec((1, PROJ_ROWS, d), lambda bi, i: (bi, i, 0)),
            full((1, d)), full(wqv_t.shape), full(w_rest.shape),
            full((HEAD_DIM, 1)), full((1, V7X_LANES)), full((1, V7X_LANES)), full((1, V7X_LANES)),
        ],
        out_specs=(
            pl.BlockSpec((1, DIFF_WIDTH, PROJ_ROWS), lambda bi, i: (bi, 0, i)),
            pl.BlockSpec((1, PROJ_ROWS // DIFF_TK, DIFF_HEADS * VT_ROWS, DIFF_TK),
                         lambda bi, i: (bi, i, 0, 0)),
            row_spec(DIFF_WIDTH),
        ) + tuple(view_spec(dil) for _, dil in DIL_PATTERNS for _ in range(3)) + (
            row_spec(MIX_WIDTH),
        ),
        out_shape=out_shape,
        scratch_shapes=[pltpu.VMEM((3 * DIL_WIDTH // V7X_LANES, PROJ_ROWS, V7X_LANES), F32)],
        compiler_params=pltpu.CompilerParams(
            dimension_semantics=("arbitrary", "arbitrary"), vmem_limit_bytes=VMEM_LIMIT),
        name="proj",
    )(x, g, wqv_t, w_rest, gq_col, gk_row, gbq_row, gbk_row)


def _diff_kernel(bound_ref, slope_ref, lam_ref, gsub_col_ref, qt_ref, k_ref, vt_ref,
                 o_ref, acc_ref, *, lam_init, n_kv):
    head = pl.program_id(1)
    iq = pl.program_id(2)
    slope = slope_ref[head]
    qt = qt_ref[0]
    row = lax.broadcasted_iota(jnp.int32, qt.shape, 0)
    zero = jnp.zeros_like(qt)
    w_half = (jnp.where(row < HEAD_DIM, qt, zero), jnp.where(row < HEAD_DIM, zero, qt))
    i0 = iq * DIFF_TQ
    shift_free = bound_ref[0] <= FAST_MAX_SCORE

    def tile_bias(j0, tk):
        rel = (lax.broadcasted_iota(jnp.int32, (tk, DIFF_TQ), 0)
               - lax.broadcasted_iota(jnp.int32, (tk, DIFF_TQ), 1) + (j0 - i0))
        return slope * jnp.abs(rel).astype(F32)

    @pl.when(shift_free)
    def _():
        tk = DIFF_FAST_TILES * DIFF_TK
        rel = (lax.broadcasted_iota(jnp.int32, (tk, DIFF_TQ), 0)
               - lax.broadcasted_iota(jnp.int32, (tk, DIFF_TQ), 1)).astype(F32)

        n_t = n_kv // DIFF_FAST_TILES

        def scores(t):
            kt = k_ref[0, t * tk:(t + 1) * tk, :]
            bias = slope * jnp.abs(rel + (t * tk - i0).astype(F32))
            return [jnp.dot(kt, w_half[c], preferred_element_type=F32) - bias for c in range(2)]

        def accumulate(t, s, acc):
            vt = jnp.concatenate(
                [vt_ref[0, t * DIFF_FAST_TILES + ch] for ch in range(DIFF_FAST_TILES)], axis=1)
            return [acc[c] + jnp.dot(vt, jnp.exp2(s[c]).astype(BF16), preferred_element_type=F32)
                    for c in range(2)]

        acc = [jnp.zeros((VT_ROWS, DIFF_TQ), F32)] * 2
        s_cur = scores(0)
        for t in range(n_t):
            s_next = scores(t + 1) if t + 1 < n_t else None
            acc = accumulate(t, s_cur, acc)
            s_cur = s_next
        for c in range(2):
            acc_ref[c] = acc[c]

    @pl.when(jnp.logical_not(shift_free))
    def _():
        acc_ref[...] = jnp.zeros_like(acc_ref)

        def body(kv, carry):
            j0 = pl.multiple_of(kv * DIFF_TK, DIFF_TK)
            kt = k_ref[0, pl.ds(j0, DIFF_TK), :]
            vt = vt_ref[0, kv]
            bias = tile_bias(j0, DIFF_TK)
            new = []
            for c in range(2):
                s = jnp.dot(kt, w_half[c], preferred_element_type=F32) - bias
                m_new = jnp.maximum(carry[c], jnp.max(s, axis=0, keepdims=True))
                alpha = jnp.exp2(carry[c] - m_new)
                p = jnp.exp2(s - m_new)
                acc_ref[c] = alpha * acc_ref[c] + jnp.dot(vt, p.astype(BF16),
                                                          preferred_element_type=F32)
                new.append(m_new)
            return tuple(new)

        init = jnp.full((1, DIFF_TQ), MASK_VALUE, F32)
        lax.fori_loop(0, n_kv, body, (init, init))

    lam_p = lam_ref[...]
    lam = (jnp.exp(jnp.sum(lam_p[0:1] * lam_p[1:2], axis=-1, keepdims=True))
           - jnp.exp(jnp.sum(lam_p[2:3] * lam_p[3:4], axis=-1, keepdims=True)) + lam_init)
    acc0, acc1 = acc_ref[0], acc_ref[1]
    a = (acc0[:DIFF_V] / acc0[DIFF_V:DIFF_V + 1]
         - lam * (acc1[:DIFF_V] / acc1[DIFF_V:DIFF_V + 1]))
    ms = jnp.mean(a * a, axis=0, keepdims=True)
    y = a * lax.rsqrt(ms + NORM_EPS) * gsub_col_ref[...] * (1.0 - lam_init)
    o_ref[0] = y.T


def _diff_attention(score_bound, slopes_l2e, lam_params, gsub_col, qt, k, vt, lam_init):
    b, s, _ = k.shape
    n_kv = s // DIFF_TK
    kern = functools.partial(_diff_kernel, lam_init=lam_init, n_kv=n_kv)
    return pl.pallas_call(
        kern,
        grid=(b, DIFF_HEADS, s // DIFF_TQ),
        in_specs=[
            pl.BlockSpec(memory_space=pltpu.SMEM),
            pl.BlockSpec(memory_space=pltpu.SMEM),
            pl.BlockSpec((4, HEAD_DIM), lambda bi, h, i: (0, 0)),
            pl.BlockSpec((DIFF_V, 1), lambda bi, h, i: (0, 0)),
            pl.BlockSpec((1, DIFF_V, DIFF_TQ), lambda bi, h, i: (bi, h, i)),
            pl.BlockSpec((1, s, DIFF_V), lambda bi, h, i: (bi, 0, h)),
            pl.BlockSpec((1, n_kv, VT_ROWS, DIFF_TK), lambda bi, h, i: (bi, 0, h, 0)),
        ],
        out_specs=pl.BlockSpec((1, DIFF_TQ, DIFF_V), lambda bi, h, i: (bi, i, h)),
        out_shape=jax.ShapeDtypeStruct((b, s, DIFF_WIDTH), F32),
        scratch_shapes=[pltpu.VMEM((2, VT_ROWS, DIFF_TQ), F32)],
        compiler_params=pltpu.CompilerParams(
            dimension_semantics=("arbitrary", "arbitrary", "arbitrary"),
            vmem_limit_bytes=VMEM_LIMIT),
        name="diff_attn",
    )(score_bound, slopes_l2e, lam_params, gsub_col, qt, k, vt)


def _dil_kernel(q_ref, kp_ref, kc_ref, kn_ref, vp_ref, vc_ref, vn_ref, o_ref, lse_ref,
                *, dilation, n_class_rows, slopes_l2e):
    i = pl.program_id(2)
    rows = DIL_ROWS
    win = rows + 2 * DIL_RADIUS
    kw = jnp.concatenate([kp_ref[0], kc_ref[0], kn_ref[0]], axis=0)
    vw = jnp.concatenate([vp_ref[0], vc_ref[0], vn_ref[0]], axis=0)
    r_idx = lax.broadcasted_iota(jnp.int32, (rows, win), 0)
    w_idx = lax.broadcasted_iota(jnp.int32, (rows, win), 1)
    rel = jnp.abs(r_idx - (w_idx - DIL_RADIUS))
    kj = i * rows - DIL_RADIUS + w_idx
    valid = (rel <= DIL_RADIUS) & (kj >= 0) & (kj < n_class_rows)
    dist = (rel * dilation).astype(F32)
    lane = lax.broadcasted_iota(jnp.int32, (rows, V7X_LANES), 1)
    lo = lane < HEAD_DIM
    for pair in range(DIL_WIDTH // V7X_LANES):
        c0 = pair * V7X_LANES
        qp = q_ref[0, :, c0:c0 + V7X_LANES]
        kp = kw[:, c0:c0 + V7X_LANES]
        vp = vw[:, c0:c0 + V7X_LANES]
        halves = []
        for e in range(2):
            qm = jnp.where(lo if e == 0 else jnp.logical_not(lo), qp, jnp.zeros_like(qp))
            s = lax.dot_general(qm, kp, (((1,), (1,)), ((), ())),
                                preferred_element_type=F32)
            s = jnp.where(valid, s - slopes_l2e[2 * pair + e] * dist, MASK_VALUE)
            m = jnp.max(s, axis=-1, keepdims=True)
            p = jnp.exp2(s - m)
            l = jnp.sum(p, axis=-1, keepdims=True)
            o = jnp.dot(p.astype(BF16), vp, preferred_element_type=F32)
            halves.append((o / l, m + jnp.log2(l)))
        o_ref[0, :, c0:c0 + V7X_LANES] = jnp.where(lo, halves[0][0], halves[1][0])
        lse_ref[0, :, c0:c0 + V7X_LANES] = jnp.where(lo, halves[0][1], halves[1][1])


def _dilated_pattern(bq, bk, bv, dilation, slopes_l2e):
    b, n, _ = bq.shape
    w = DIL_WIDTH
    s = n * dilation
    halo_per_block = DIL_ROWS // DIL_RADIUS
    n_halo = n // DIL_RADIUS
    cur = pl.BlockSpec((1, DIL_ROWS, w), lambda bi, c, i: (bi, i, c))
    prev = pl.BlockSpec((1, DIL_RADIUS, w),
                        lambda bi, c, i: (bi, jnp.maximum(i * halo_per_block - 1, 0), c))
    nxt = pl.BlockSpec((1, DIL_RADIUS, w),
                       lambda bi, c, i: (bi, jnp.minimum((i + 1) * halo_per_block, n_halo - 1), c))
    kern = functools.partial(_dil_kernel, dilation=dilation, n_class_rows=n,
                             slopes_l2e=slopes_l2e)
    o, lse = pl.pallas_call(
        kern,
        grid=(b, dilation, n // DIL_ROWS),
        in_specs=[cur, prev, cur, nxt, prev, cur, nxt],
        out_specs=(cur, cur),
        out_shape=(jax.ShapeDtypeStruct((b, n, dilation * w), F32),
                   jax.ShapeDtypeStruct((b, n, dilation * w), F32)),
        compiler_params=pltpu.CompilerParams(
            dimension_semantics=("arbitrary", "arbitrary", "arbitrary"),
            vmem_limit_bytes=VMEM_LIMIT),
        name=f"dil_attn_d{dilation}",
    )(bq, bk, bk, bk, bv, bv, bv)
    return o.reshape(b, s, w), lse.reshape(b, s, w)


def _dil_merge_kernel(o1_ref, o2_ref, o3_ref, l1_ref, l2_ref, l3_ref, b_ref):
    l1, l2, l3 = l1_ref[0], l2_ref[0], l3_ref[0]
    m = jnp.maximum(jnp.maximum(l1, l2), l3)
    w1, w2, w3 = jnp.exp2(l1 - m), jnp.exp2(l2 - m), jnp.exp2(l3 - m)
    b_ref[0] = (w1 * o1_ref[0] + w2 * o2_ref[0] + w3 * o3_ref[0]) / (w1 + w2 + w3)


def _dilated_by_pattern(views, slopes_l2e):
    outs, lses = [], []
    for (_, dilation), (bq, bk, bv) in zip(DIL_PATTERNS, views):
        o, lse = _dilated_pattern(bq, bk, bv, dilation, slopes_l2e)
        outs.append(o)
        lses.append(lse)
    b, s, w = outs[0].shape
    spec = pl.BlockSpec((1, OUT_ROWS, w), lambda bi, i: (bi, i, 0))
    return pl.pallas_call(
        _dil_merge_kernel,
        grid=(b, s // OUT_ROWS),
        in_specs=[spec] * 6,
        out_specs=spec,
        out_shape=jax.ShapeDtypeStruct((b, s, w), F32),
        compiler_params=pltpu.CompilerParams(
            dimension_semantics=("arbitrary", "arbitrary"), vmem_limit_bytes=VMEM_LIMIT),
        name="dil_merge",
    )(*outs, *lses)


def _dil_fused_kernel(*refs, slopes_l2e):
    q_refs = refs[0:3]
    k_refs = [refs[3 + 3 * n:6 + 3 * n] for n in range(3)]
    v_refs = [refs[12 + 3 * n:15 + 3 * n] for n in range(3)]
    o_ref, tab_ref, kw_ref, vw_ref, acc_ref, l_ref = refs[21:]
    bi, i = pl.program_id(0), pl.program_id(1)
    rows, win, rad = DIL_Q, DIL_Q + 2 * DIL_RADIUS, DIL_RADIUS
    n_col = DIL_WIDTH // V7X_LANES

    @pl.when(jnp.logical_and(bi == 0, i == 0))
    def _():
        rel = jnp.abs(lax.broadcasted_iota(jnp.int32, (rows, win), 0)
                      - lax.broadcasted_iota(jnp.int32, (rows, win), 1) + rad)
        for n, (_, dilation) in enumerate(DIL_PATTERNS):
            dist = (rel * dilation).astype(F32)
            for hd in range(DIL_HEADS):
                tab_ref[n, hd // 2, (hd % 2) * rows:(hd % 2 + 1) * rows, :] = jnp.where(
                    rel <= rad, -slopes_l2e[hd] * dist, MASK_VALUE)

    col = lax.broadcasted_iota(jnp.int32, (1, win), 1)
    no_prev = jnp.where(jnp.logical_and(col < rad, i == 0), MASK_VALUE, 0.0)
    ones = jnp.ones((win, V7X_LANES), BF16)

    def no_next(start):
        return jnp.where(jnp.logical_and(col >= start, i == pl.num_programs(1) - 1),
                         MASK_VALUE, 0.0)

    def head_pair(qp, kwin, vwin, bias):
        r = qp.shape[0]
        lo = lax.broadcasted_iota(jnp.int32, (r, V7X_LANES), 1) < HEAD_DIM
        zq = jnp.zeros_like(qp)
        lhs = jnp.concatenate([jnp.where(lo, qp, zq), jnp.where(lo, zq, qp)], axis=0)
        s = lax.dot_general(lhs, kwin, (((1,), (1,)), ((), ())),
                            preferred_element_type=F32) + bias
        o = jnp.dot(jnp.exp2(s).astype(BF16), jnp.concatenate([vwin, ones], axis=1),
                    preferred_element_type=F32)
        return (jnp.where(lo, o[:r, :V7X_LANES], o[r:, :V7X_LANES]),
                jnp.where(lo, o[:r, V7X_LANES:], o[r:, V7X_LANES:]))

    def fill_window(n, cls, n_cur):
        cs = slice(cls * DIL_WIDTH, (cls + 1) * DIL_WIDTH)
        for w_ref, (p_ref, c_ref, n_ref) in ((kw_ref, k_refs[n]), (vw_ref, v_refs[n])):
            w_ref[0:rad, :] = p_ref[0, :, cs]
            w_ref[rad:rad + n_cur, :] = c_ref[0, :, cs]
            w_ref[rad + n_cur:2 * rad + n_cur, :] = n_ref[0, :, cs]

    n_sub = DIL_SB // rows
    fill_window(0, 0, DIL_SB)

    def body(j, carry):
        r0 = pl.multiple_of(j * rows, rows)
        edge = jnp.where(j == 0, no_prev, 0.0) + jnp.where(j == n_sub - 1, no_next(win - rad), 0.0)
        for g in range(n_col):
            cs = slice(g * V7X_LANES, (g + 1) * V7X_LANES)
            acc, l = head_pair(q_refs[0][0, pl.ds(r0, rows), cs], kw_ref[pl.ds(r0, win), cs],
                               vw_ref[pl.ds(r0, win), cs], tab_ref[0, g] + edge)
            acc_ref[g, pl.ds(r0, rows), :] = acc
            l_ref[g, pl.ds(r0, rows), :] = l
        return carry

    lax.fori_loop(0, n_sub, body, 0)

    for n, (_, dilation) in list(enumerate(DIL_PATTERNS))[1:]:
        n_cur = DIL_SB // dilation
        r = min(rows, n_cur)
        n_sub = n_cur // r
        for cls in range(dilation):
            fill_window(n, cls, n_cur)
            for j in range(n_sub):
                edge = ((no_prev if j == 0 else 0.0)
                        + (no_next(r + rad) if j == n_sub - 1 else 0.0))
                for g in range(n_col):
                    cs = slice(g * V7X_LANES, (g + 1) * V7X_LANES)
                    qs = slice(cls * DIL_WIDTH + g * V7X_LANES, cls * DIL_WIDTH + (g + 1) * V7X_LANES)
                    bias = jnp.concatenate([tab_ref[n, g, 0:r, :], tab_ref[n, g, rows:rows + r, :]],
                                           axis=0) + edge
                    acc, l = head_pair(q_refs[n][0, j * r:(j + 1) * r, qs],
                                       kw_ref[j * r:j * r + win, cs], vw_ref[j * r:j * r + win, cs],
                                       bias)
                    where = pl.ds(j * r * dilation + cls, r, stride=dilation)
                    acc_ref[g, where, :] += acc
                    l_ref[g, where, :] += l

    for g in range(n_col):
        o_ref[0, :, g * V7X_LANES:(g + 1) * V7X_LANES] = acc_ref[g] / l_ref[g]


def _dilated_fused(views, slopes_l2e):
    b = views[0][0].shape[0]
    s = views[0][0].shape[1]
    in_specs, args = [], []
    for (_, dilation), (bq, _, _) in zip(DIL_PATTERNS, views):
        in_specs.append(pl.BlockSpec((1, DIL_SB // dilation, dilation * DIL_WIDTH),
                                     lambda bi, i: (bi, i, 0)))
        args.append(bq)
    for tns in (1, 2):
        for (_, dilation), view in zip(DIL_PATTERNS, views):
            n_cur = DIL_SB // dilation
            per_block = n_cur // DIL_RADIUS
            n_halo = s // dilation // DIL_RADIUS
            in_specs += [
                pl.BlockSpec((1, DIL_RADIUS, dilation * DIL_WIDTH),
                             lambda bi, i, pb=per_block: (bi, jnp.maximum(i * pb - 1, 0), 0)),
                pl.BlockSpec((1, n_cur, dilation * DIL_WIDTH), lambda bi, i: (bi, i, 0)),
                pl.BlockSpec((1, DIL_RADIUS, dilation * DIL_WIDTH),
                             lambda bi, i, pb=per_block, nh=n_halo:
                             (bi, jnp.minimum((i + 1) * pb, nh - 1), 0)),
            ]
            args += [view[tns]] * 3
    n_col = DIL_WIDTH // V7X_LANES
    return pl.pallas_call(
        functools.partial(_dil_fused_kernel, slopes_l2e=slopes_l2e),
        grid=(b, s // DIL_SB),
        in_specs=in_specs,
        out_specs=pl.BlockSpec((1, DIL_SB, DIL_WIDTH), lambda bi, i: (bi, i, 0)),
        out_shape=jax.ShapeDtypeStruct((b, s, DIL_WIDTH), F32),
        scratch_shapes=[
            pltpu.VMEM((len(DIL_PATTERNS), n_col, 2 * DIL_Q, DIL_Q + 2 * DIL_RADIUS), F32),
            pltpu.VMEM((DIL_SB + 2 * DIL_RADIUS, DIL_WIDTH), BF16),
            pltpu.VMEM((DIL_SB + 2 * DIL_RADIUS, DIL_WIDTH), BF16),
            pltpu.VMEM((n_col, DIL_SB, V7X_LANES), F32),
            pltpu.VMEM((n_col, DIL_SB, V7X_LANES), F32),
        ],
        compiler_params=pltpu.CompilerParams(
            dimension_semantics=("arbitrary", "arbitrary"), vmem_limit_bytes=DIL_VMEM_LIMIT),
        name="dil_fused",
    )(*args)


def _out_kernel(x_ref, p_ref, a_ref, sz_ref, b_ref, wout_ref, g2_ref, wpg_ref, wpp_ref, y_ref):
    sz = sz_ref[0]
    ya = (a_ref[0] * sz[:, :DIFF_WIDTH]).astype(BF16)
    yb = (b_ref[0] * sz[:, DIFF_WIDTH:]).astype(BF16)
    x1 = (x_ref[0]
          + jnp.dot(ya, wout_ref[:DIFF_WIDTH, :], preferred_element_type=F32)
          + jnp.dot(yb, wout_ref[DIFF_WIDTH:, :], preferred_element_type=F32))
    ms = jnp.mean(x1 * x1, axis=-1, keepdims=True)
    h2 = (x1 * lax.rsqrt(ms + NORM_EPS) * g2_ref[...]).astype(BF16)
    gate = jax.nn.sigmoid(jnp.dot(h2, wpg_ref[...], preferred_element_type=F32))
    pp = jnp.dot(p_ref[0].astype(BF16), wpp_ref[...], preferred_element_type=F32)
    y_ref[0] = x1 + gate * pp


def _out(x, p, a, sz, bmix, wout, g2, wpg, wpp):
    b, s, d = x.shape
    full = lambda shape: pl.BlockSpec(shape, lambda bi, i: (0,) * len(shape))
    row_spec = lambda w: pl.BlockSpec((1, OUT_ROWS, w), lambda bi, i: (bi, i, 0))
    return pl.pallas_call(
        _out_kernel,
        grid=(b, s // OUT_ROWS),
        in_specs=[row_spec(d), row_spec(p.shape[-1]), row_spec(DIFF_WIDTH), row_spec(MIX_WIDTH),
                  row_spec(DIL_WIDTH),
                  full(wout.shape), full((1, d)), full(wpg.shape), full(wpp.shape)],
        out_specs=row_spec(d),
        out_shape=jax.ShapeDtypeStruct((b, s, d), F32),
        compiler_params=pltpu.CompilerParams(
            dimension_semantics=("arbitrary", "arbitrary"), vmem_limit_bytes=VMEM_LIMIT),
        name="out_proj",
    )(x, p, a, sz, bmix, wout, g2, wpg, wpp)


def kernel(x, p, mix_norm_g, w_in, diff_q_norm_g, diff_k_norm_g, lambda_q1, lambda_k1,
           lambda_q2, lambda_k2, diff_sub_norm_g, dil_q_norm_g, dil_k_norm_g, w_out,
           ple_norm_g, w_ple_gate, w_ple_proj):
    depth = w_in.shape[0]
    diff_slopes = _alibi_slopes(DIFF_HEADS) * LOG2E
    dil_slopes = tuple(float(2.0 ** (-8.0 * (i + 1) / DIL_HEADS)) * LOG2E for i in range(DIL_HEADS))
    fold = QK_SCALE * LOG2E
    for i in range(depth):
        lam_init = 0.8 - 0.6 * math.exp(-0.3 * i)
        w = w_in[i]
        wqv_t = jnp.concatenate([w[:, :512], w[:, 1024:1536]], axis=1).T.astype(BF16)
        w_rest = jnp.concatenate([w[:, 512:1024], w[:, 1536:]], axis=1).astype(BF16)
        row2 = lambda g: jnp.tile(g, 2)[None, :]
        qt, vt, k, *dil_qkv, sz = _proj(
            x, mix_norm_g[i][None, :], wqv_t, w_rest,
            (diff_q_norm_g[i] * fold)[:, None], row2(diff_k_norm_g[i]),
            row2(dil_q_norm_g[i] * fold), row2(dil_k_norm_g[i]))
        views = tuple(tuple(dil_qkv[3 * n:3 * n + 3]) for n in range(len(DIL_PATTERNS)))
        lam_params = jnp.stack([lambda_q1[i], lambda_k1[i], lambda_q2[i], lambda_k2[i]])

        def score_bound(gq, gk):
            return 1.01 * HEAD_DIM * fold * jnp.max(jnp.abs(gq)) * jnp.max(jnp.abs(gk))

        a = _diff_attention(score_bound(diff_q_norm_g[i], diff_k_norm_g[i]).reshape(1), diff_slopes,
                            lam_params, diff_sub_norm_g[i][:, None], qt, k, vt, lam_init)
        bmix = lax.cond(score_bound(dil_q_norm_g[i], dil_k_norm_g[i]) <= FAST_MAX_SCORE,
                        lambda v: _dilated_fused(v, dil_slopes),
                        lambda v: _dilated_by_pattern(v, dil_slopes), views)
        x = _out(x, p[i], a, sz, bmix, w_out[i].astype(BF16), ple_norm_g[i][None, :],
                 w_ple_gate[i].astype(BF16), w_ple_proj[i].astype(BF16))
    return x
```

```python
import functools
import math

import jax
import jax.numpy as jnp
from jax import lax
from jax.experimental import pallas as pl
from jax.experimental.pallas import tpu as pltpu

F32 = jnp.float32
BF16 = jnp.bfloat16

HEAD_DIM = 64
DIFF_HEADS = 4
DIFF_V = 2 * HEAD_DIM
DIFF_WIDTH = DIFF_HEADS * DIFF_V
DIL_HEADS = 8
DIL_WIDTH = DIL_HEADS * HEAD_DIM
DIL_PATTERNS = ((128, 1), (512, 4), (2048, 16))
MIX_WIDTH = DIFF_WIDTH + DIL_WIDTH
NORM_EPS = 1e-6
MASK_VALUE = -1e30
LOG2E = math.log2(math.e)
QK_SCALE = HEAD_DIM ** -0.5

V7X_LANES = 128
PROJ_ROWS = 512
DIFF_TQ = 256
DIFF_TK = 512
DIFF_FAST_TILES = 2
VT_ROWS = DIFF_V + 16
FAST_MAX_SCORE = 40.0
DIL_ROWS = 256
DIL_SB = 1024
DIL_Q = 128
DIL_UNROLL = 4
DIL_VMEM_LIMIT = 56 * 1024 * 1024
DIL_RADIUS = 64
OUT_ROWS = 512
VMEM_LIMIT = 48 * 1024 * 1024


def _alibi_slopes(n_heads):
    return jnp.exp2(-8.0 * jnp.arange(1, n_heads + 1, dtype=F32) / n_heads)


def _pair_head_norm(t, gain_row):
    lane = lax.broadcasted_iota(jnp.int32, t.shape, 1)
    lo = lane < HEAD_DIM
    sq = t * t
    ms_lo = jnp.sum(jnp.where(lo, sq, 0.0), axis=-1, keepdims=True) * (1.0 / HEAD_DIM)
    ms_hi = jnp.sum(jnp.where(lo, 0.0, sq), axis=-1, keepdims=True) * (1.0 / HEAD_DIM)
    r = jnp.where(lo, lax.rsqrt(ms_lo + NORM_EPS), lax.rsqrt(ms_hi + NORM_EPS))
    return t * r * gain_row


def _proj_kernel(x_ref, g_ref, wqv_t_ref, w_ref, gq_col_ref, gk_row_ref,
                 gbq_row_ref, gbk_row_ref,
                 qt_ref, vt_ref, k_ref, bq_ref, bk_ref, bv_ref,
                 bq4_ref, bk4_ref, bv4_ref, bq16_ref, bk16_ref, bv16_ref, sz_ref, ub_ref, ub4_ref):
    x = x_ref[0]
    ms = jnp.mean(x * x, axis=-1, keepdims=True)
    h = (x * lax.rsqrt(ms + NORM_EPS) * g_ref[...]).astype(BF16)


    ub = jnp.dot(h, w_ref[:, 512:2048], preferred_element_type=F32)
    n_col = DIL_WIDTH // V7X_LANES
    for pair in range(n_col):
        c0 = pair * V7X_LANES
        ub_ref[pair] = _pair_head_norm(ub[:, c0:c0 + V7X_LANES], gbq_row_ref[...])
        ub_ref[n_col + pair] = _pair_head_norm(ub[:, 512 + c0:512 + c0 + V7X_LANES],
                                               gbk_row_ref[...])
        ub_ref[2 * n_col + pair] = ub[:, 1024 + c0:1024 + c0 + V7X_LANES]
    nat_refs, d4_refs, d16_refs = ((bq_ref, bk_ref, bv_ref), (bq4_ref, bk4_ref, bv4_ref),
                                   (bq16_ref, bk16_ref, bv16_ref))
    n4 = PROJ_ROWS // 4
    for tns in range(3):
        for pair in range(n_col):
            slab = tns * n_col + pair
            c0 = pair * V7X_LANES
            nat_refs[tns][0, :, c0:c0 + V7X_LANES] = ub_ref[slab].astype(BF16)
            for c4 in range(4):
                rows4 = ub_ref[slab, pl.ds(c4, n4, stride=4), :]
                d4_refs[tns][0, :, c4 * DIL_WIDTH + c0:c4 * DIL_WIDTH + c0 + V7X_LANES] = (
                    rows4.astype(BF16))
                ub4_ref[slab, c4] = rows4
            for c4 in range(4):
                for e in range(4):
                    rows16 = ub4_ref[slab, c4, pl.ds(e, n4 // 4, stride=4), :]
                    cb = (4 * e + c4) * DIL_WIDTH + c0
                    d16_refs[tns][0, :, cb:cb + V7X_LANES] = rows16.astype(BF16)

    uk = jnp.dot(h, w_ref[:, 0:512], preferred_element_type=F32)
    for pair in range(DIFF_WIDTH // V7X_LANES):
        c0 = pair * V7X_LANES
        k_ref[0, :, c0:c0 + V7X_LANES] = _pair_head_norm(
            uk[:, c0:c0 + V7X_LANES], gk_row_ref[...]).astype(BF16)

    nt_dims = (((1,), (1,)), ((), ()))
    uqt = lax.dot_general(wqv_t_ref[0:DIFF_WIDTH, :], h, nt_dims, preferred_element_type=F32)
    for grp in range(DIFF_WIDTH // HEAD_DIM):
        t = uqt[grp * HEAD_DIM:(grp + 1) * HEAD_DIM, :]
        msq = jnp.mean(t * t, axis=0, keepdims=True)
        y = t * lax.rsqrt(msq + NORM_EPS) * gq_col_ref[...]
        qt_ref[0, grp * HEAD_DIM:(grp + 1) * HEAD_DIM, :] = y.astype(BF16)

    z = jnp.dot(h, w_ref[:, 2048:], preferred_element_type=F32)
    sz_ref[0] = (z * jax.nn.sigmoid(z)).astype(sz_ref.dtype)

    vt = lax.dot_general(wqv_t_ref[DIFF_WIDTH:, :], h, nt_dims,
                         preferred_element_type=F32).astype(BF16)
    ones = jnp.ones((VT_ROWS - DIFF_V, DIFF_TK), BF16)
    for j in range(PROJ_ROWS // DIFF_TK):
        for hd in range(DIFF_HEADS):
            vt_ref[0, j, hd * VT_ROWS:hd * VT_ROWS + DIFF_V, :] = (
                vt[hd * DIFF_V:(hd + 1) * DIFF_V, j * DIFF_TK:(j + 1) * DIFF_TK])
            vt_ref[0, j, hd * VT_ROWS + DIFF_V:(hd + 1) * VT_ROWS, :] = ones


def _proj(x, g, wqv_t, w_rest, gq_col, gk_row, gbq_row, gbk_row):
    b, s, d = x.shape
    n_row = s // PROJ_ROWS
    full = lambda shape: pl.BlockSpec(shape, lambda bi, i: (0,) * len(shape))
    out_shape = (
        jax.ShapeDtypeStruct((b, DIFF_WIDTH, s), BF16),
        jax.ShapeDtypeStruct((b, s // DIFF_TK, DIFF_HEADS * VT_ROWS, DIFF_TK), BF16),
        jax.ShapeDtypeStruct((b, s, DIFF_WIDTH), BF16),
    ) + tuple(
        jax.ShapeDtypeStruct((b, s // dil, dil * DIL_WIDTH), BF16)
        for _, dil in DIL_PATTERNS for _ in range(3)
    ) + (jax.ShapeDtypeStruct((b, s, MIX_WIDTH), BF16),)
    row_spec = lambda w: pl.BlockSpec((1, PROJ_ROWS, w), lambda bi, i: (bi, i, 0))
    view_spec = lambda dil: pl.BlockSpec((1, PROJ_ROWS // dil, dil * DIL_WIDTH),
                                         lambda bi, i: (bi, i, 0))
    return pl.pallas_call(
        _proj_kernel,
        grid=(b, n_row),
        in_specs=[
            pl.BlockSpec((1, PROJ_ROWS, d), lambda bi, i: (bi, i, 0)),
            full((1, d)), full(wqv_t.shape), full(w_rest.shape),
            full((HEAD_DIM, 1)), full((1, V7X_LANES)), full((1, V7X_LANES)), full((1, V7X_LANES)),
        ],
        out_specs=(
            pl.BlockSpec((1, DIFF_WIDTH, PROJ_ROWS), lambda bi, i: (bi, 0, i)),
            pl.BlockSpec((1, PROJ_ROWS // DIFF_TK, DIFF_HEADS * VT_ROWS, DIFF_TK),
                         lambda bi, i: (bi, i, 0, 0)),
            row_spec(DIFF_WIDTH),
        ) + tuple(view_spec(dil) for _, dil in DIL_PATTERNS for _ in range(3)) + (
            row_spec(MIX_WIDTH),
        ),
        out_shape=out_shape,
        scratch_shapes=[pltpu.VMEM((3 * DIL_WIDTH // V7X_LANES, PROJ_ROWS, V7X_LANES), F32),
                        pltpu.VMEM((3 * DIL_WIDTH // V7X_LANES, 4, PROJ_ROWS // 4, V7X_LANES), F32)],
        compiler_params=pltpu.CompilerParams(
            dimension_semantics=("arbitrary", "arbitrary"), vmem_limit_bytes=VMEM_LIMIT),
        name="proj",
    )(x, g, wqv_t, w_rest, gq_col, gk_row, gbq_row, gbk_row)


def _diff_kernel(bound_ref, slope_ref, lam_ref, gsub_col_ref, qt_ref, k_ref, vt_ref,
                 o_ref, acc_ref, *, lam_init, n_kv):
    head = pl.program_id(1)
    iq = pl.program_id(2)
    slope = slope_ref[head]
    qt = qt_ref[0]
    row = lax.broadcasted_iota(jnp.int32, qt.shape, 0)
    zero = jnp.zeros_like(qt)
    w_half = (jnp.where(row < HEAD_DIM, qt, zero), jnp.where(row < HEAD_DIM, zero, qt))
    i0 = iq * DIFF_TQ
    shift_free = bound_ref[0] <= FAST_MAX_SCORE

    def tile_bias(j0, tk):
        rel = (lax.broadcasted_iota(jnp.int32, (tk, DIFF_TQ), 0)
               - lax.broadcasted_iota(jnp.int32, (tk, DIFF_TQ), 1) + (j0 - i0))
        return slope * jnp.abs(rel).astype(F32)

    @pl.when(shift_free)
    def _():
        tk = DIFF_FAST_TILES * DIFF_TK
        rel = (lax.broadcasted_iota(jnp.int32, (tk, DIFF_TQ), 0)
               - lax.broadcasted_iota(jnp.int32, (tk, DIFF_TQ), 1)).astype(F32)

        n_t = n_kv // DIFF_FAST_TILES

        def scores(t):
            kt = k_ref[0, t * tk:(t + 1) * tk, :]
            bias = slope * jnp.abs(rel + (t * tk - i0).astype(F32))
            return [jnp.dot(kt, w_half[c], preferred_element_type=F32) - bias for c in range(2)]

        def accumulate(t, s, acc):
            vt = jnp.concatenate(
                [vt_ref[0, t * DIFF_FAST_TILES + ch] for ch in range(DIFF_FAST_TILES)], axis=1)
            return [acc[c] + jnp.dot(vt, jnp.exp2(s[c]).astype(BF16), preferred_element_type=F32)
                    for c in range(2)]

        acc = [jnp.zeros((VT_ROWS, DIFF_TQ), F32)] * 2
        s_cur = scores(0)
        for t in range(n_t):
            s_next = scores(t + 1) if t + 1 < n_t else None
            acc = accumulate(t, s_cur, acc)
            s_cur = s_next
        for c in range(2):
            acc_ref[c] = acc[c]

    @pl.when(jnp.logical_not(shift_free))
    def _():
        acc_ref[...] = jnp.zeros_like(acc_ref)

        def body(kv, carry):
            j0 = pl.multiple_of(kv * DIFF_TK, DIFF_TK)
            kt = k_ref[0, pl.ds(j0, DIFF_TK), :]
            vt = vt_ref[0, kv]
            bias = tile_bias(j0, DIFF_TK)
            new = []
            for c in range(2):
                s = jnp.dot(kt, w_half[c], preferred_element_type=F32) - bias
                m_new = jnp.maximum(carry[c], jnp.max(s, axis=0, keepdims=True))
                alpha = jnp.exp2(carry[c] - m_new)
                p = jnp.exp2(s - m_new)
                acc_ref[c] = alpha * acc_ref[c] + jnp.dot(vt, p.astype(BF16),
                                                          preferred_element_type=F32)
                new.append(m_new)
            return tuple(new)

        init = jnp.full((1, DIFF_TQ), MASK_VALUE, F32)
        lax.fori_loop(0, n_kv, body, (init, init))

    lam_p = lam_ref[...]
    lam = (jnp.exp(jnp.sum(lam_p[0:1] * lam_p[1:2], axis=-1, keepdims=True))
           - jnp.exp(jnp.sum(lam_p[2:3] * lam_p[3:4], axis=-1, keepdims=True)) + lam_init)
    acc0, acc1 = acc_ref[0], acc_ref[1]
    a = (acc0[:DIFF_V] / acc0[DIFF_V:DIFF_V + 1]
         - lam * (acc1[:DIFF_V] / acc1[DIFF_V:DIFF_V + 1]))
    ms = jnp.mean(a * a, axis=0, keepdims=True)
    y = a * lax.rsqrt(ms + NORM_EPS) * gsub_col_ref[...] * (1.0 - lam_init)
    o_ref[0] = y.T.astype(o_ref.dtype)


def _diff_attention(score_bound, slopes_l2e, lam_params, gsub_col, qt, k, vt, lam_init):
    b, s, _ = k.shape
    n_kv = s // DIFF_TK
    kern = functools.partial(_diff_kernel, lam_init=lam_init, n_kv=n_kv)
    return pl.pallas_call(
        kern,
        grid=(b, DIFF_HEADS, s // DIFF_TQ),
        in_specs=[
            pl.BlockSpec(memory_space=pltpu.SMEM),
            pl.BlockSpec(memory_space=pltpu.SMEM),
            pl.BlockSpec((4, HEAD_DIM), lambda bi, h, i: (0, 0)),
            pl.BlockSpec((DIFF_V, 1), lambda bi, h, i: (0, 0)),
            pl.BlockSpec((1, DIFF_V, DIFF_TQ), lambda bi, h, i: (bi, h, i)),
            pl.BlockSpec((1, s, DIFF_V), lambda bi, h, i: (bi, 0, h)),
            pl.BlockSpec((1, n_kv, VT_ROWS, DIFF_TK), lambda bi, h, i: (bi, 0, h, 0)),
        ],
        out_specs=pl.BlockSpec((1, DIFF_TQ, DIFF_V), lambda bi, h, i: (bi, i, h)),
        out_shape=jax.ShapeDtypeStruct((b, s, DIFF_WIDTH), BF16),
        scratch_shapes=[pltpu.VMEM((2, VT_ROWS, DIFF_TQ), F32)],
        compiler_params=pltpu.CompilerParams(
            dimension_semantics=("arbitrary", "arbitrary", "arbitrary"),
            vmem_limit_bytes=VMEM_LIMIT),
        name="diff_attn",
    )(score_bound, slopes_l2e, lam_params, gsub_col, qt, k, vt)


def _dil_kernel(q_ref, kp_ref, kc_ref, kn_ref, vp_ref, vc_ref, vn_ref, o_ref, lse_ref,
                *, dilation, n_class_rows, slopes_l2e):
    i = pl.program_id(2)
    rows = DIL_ROWS
    win = rows + 2 * DIL_RADIUS
    kw = jnp.concatenate([kp_ref[0], kc_ref[0], kn_ref[0]], axis=0)
    vw = jnp.concatenate([vp_ref[0], vc_ref[0], vn_ref[0]], axis=0)
    r_idx = lax.broadcasted_iota(jnp.int32, (rows, win), 0)
    w_idx = lax.broadcasted_iota(jnp.int32, (rows, win), 1)
    rel = jnp.abs(r_idx - (w_idx - DIL_RADIUS))
    kj = i * rows - DIL_RADIUS + w_idx
    valid = (rel <= DIL_RADIUS) & (kj >= 0) & (kj < n_class_rows)
    dist = (rel * dilation).astype(F32)
    lane = lax.broadcasted_iota(jnp.int32, (rows, V7X_LANES), 1)
    lo = lane < HEAD_DIM
    for pair in range(DIL_WIDTH // V7X_LANES):
        c0 = pair * V7X_LANES
        qp = q_ref[0, :, c0:c0 + V7X_LANES]
        kp = kw[:, c0:c0 + V7X_LANES]
        vp = vw[:, c0:c0 + V7X_LANES]
        halves = []
        for e in range(2):
            qm = jnp.where(lo if e == 0 else jnp.logical_not(lo), qp, jnp.zeros_like(qp))
            s = lax.dot_general(qm, kp, (((1,), (1,)), ((), ())),
                                preferred_element_type=F32)
            s = jnp.where(valid, s - slopes_l2e[2 * pair + e] * dist, MASK_VALUE)
            m = jnp.max(s, axis=-1, keepdims=True)
            p = jnp.exp2(s - m)
            l = jnp.sum(p, axis=-1, keepdims=True)
            o = jnp.dot(p.astype(BF16), vp, preferred_element_type=F32)
            halves.append((o / l, m + jnp.log2(l)))
        o_ref[0, :, c0:c0 + V7X_LANES] = jnp.where(lo, halves[0][0], halves[1][0])
        lse_ref[0, :, c0:c0 + V7X_LANES] = jnp.where(lo, halves[0][1], halves[1][1])


def _dilated_pattern(bq, bk, bv, dilation, slopes_l2e):
    b, n, _ = bq.shape
    w = DIL_WIDTH
    s = n * dilation
    halo_per_block = DIL_ROWS // DIL_RADIUS
    n_halo = n // DIL_RADIUS
    cur = pl.BlockSpec((1, DIL_ROWS, w), lambda bi, c, i: (bi, i, c))
    prev = pl.BlockSpec((1, DIL_RADIUS, w),
                        lambda bi, c, i: (bi, jnp.maximum(i * halo_per_block - 1, 0), c))
    nxt = pl.BlockSpec((1, DIL_RADIUS, w),
                       lambda bi, c, i: (bi, jnp.minimum((i + 1) * halo_per_block, n_halo - 1), c))
    kern = functools.partial(_dil_kernel, dilation=dilation, n_class_rows=n,
                             slopes_l2e=slopes_l2e)
    o, lse = pl.pallas_call(
        kern,
        grid=(b, dilation, n // DIL_ROWS),
        in_specs=[cur, prev, cur, nxt, prev, cur, nxt],
        out_specs=(cur, cur),
        out_shape=(jax.ShapeDtypeStruct((b, n, dilation * w), F32),
                   jax.ShapeDtypeStruct((b, n, dilation * w), F32)),
        compiler_params=pltpu.CompilerParams(
            dimension_semantics=("arbitrary", "arbitrary", "arbitrary"),
            vmem_limit_bytes=VMEM_LIMIT),
        name=f"dil_attn_d{dilation}",
    )(bq, bk, bk, bk, bv, bv, bv)
    return o.reshape(b, s, w), lse.reshape(b, s, w)


def _dil_merge_kernel(o1_ref, o2_ref, o3_ref, l1_ref, l2_ref, l3_ref, b_ref):
    l1, l2, l3 = l1_ref[0], l2_ref[0], l3_ref[0]
    m = jnp.maximum(jnp.maximum(l1, l2), l3)
    w1, w2, w3 = jnp.exp2(l1 - m), jnp.exp2(l2 - m), jnp.exp2(l3 - m)
    b_ref[0] = ((w1 * o1_ref[0] + w2 * o2_ref[0] + w3 * o3_ref[0])
                / (w1 + w2 + w3)).astype(b_ref.dtype)


def _dilated_by_pattern(views, slopes_l2e):
    outs, lses = [], []
    for (_, dilation), (bq, bk, bv) in zip(DIL_PATTERNS, views):
        o, lse = _dilated_pattern(bq, bk, bv, dilation, slopes_l2e)
        outs.append(o)
        lses.append(lse)
    b, s, w = outs[0].shape
    spec = pl.BlockSpec((1, OUT_ROWS, w), lambda bi, i: (bi, i, 0))
    return pl.pallas_call(
        _dil_merge_kernel,
        grid=(b, s // OUT_ROWS),
        in_specs=[spec] * 6,
        out_specs=spec,
        out_shape=jax.ShapeDtypeStruct((b, s, w), BF16),
        compiler_params=pltpu.CompilerParams(
            dimension_semantics=("arbitrary", "arbitrary"), vmem_limit_bytes=VMEM_LIMIT),
        name="dil_merge",
    )(*outs, *lses)


def _dil_fused_kernel(*refs, slopes_l2e):
    q_refs = refs[0:3]
    k_refs = [refs[3 + 3 * n:6 + 3 * n] for n in range(3)]
    v_refs = [refs[12 + 3 * n:15 + 3 * n] for n in range(3)]
    o_ref, tab_ref, kw_ref, vw_ref, acc_ref, l_ref = refs[21:]
    bi, i = pl.program_id(0), pl.program_id(1)
    rows, win, rad = DIL_Q, DIL_Q + 2 * DIL_RADIUS, DIL_RADIUS
    n_col = DIL_WIDTH // V7X_LANES

    @pl.when(jnp.logical_and(bi == 0, i == 0))
    def _():
        rel = jnp.abs(lax.broadcasted_iota(jnp.int32, (rows, win), 0)
                      - lax.broadcasted_iota(jnp.int32, (rows, win), 1) + rad)
        for n, (_, dilation) in enumerate(DIL_PATTERNS):
            dist = (rel * dilation).astype(F32)
            for hd in range(DIL_HEADS):
                tab_ref[n, hd // 2, (hd % 2) * rows:(hd % 2 + 1) * rows, :] = jnp.where(
                    rel <= rad, -slopes_l2e[hd] * dist, MASK_VALUE)

    col = lax.broadcasted_iota(jnp.int32, (1, win), 1)
    no_prev = jnp.where(jnp.logical_and(col < rad, i == 0), MASK_VALUE, 0.0)
    ones = jnp.ones((win, V7X_LANES), BF16)

    def no_next(start):
        return jnp.where(jnp.logical_and(col >= start, i == pl.num_programs(1) - 1),
                         MASK_VALUE, 0.0)

    def head_pair(qp, kwin, vwin, bias):
        r = qp.shape[0]
        lo = lax.broadcasted_iota(jnp.int32, (r, V7X_LANES), 1) < HEAD_DIM
        zq = jnp.zeros_like(qp)
        lhs = jnp.concatenate([jnp.where(lo, qp, zq), jnp.where(lo, zq, qp)], axis=0)
        s = lax.dot_general(lhs, kwin, (((1,), (1,)), ((), ())),
                            preferred_element_type=F32) + bias
        o = jnp.dot(jnp.exp2(s).astype(BF16), jnp.concatenate([vwin, ones], axis=1),
                    preferred_element_type=F32)
        return (jnp.where(lo, o[:r, :V7X_LANES], o[r:, :V7X_LANES]),
                jnp.where(lo, o[:r, V7X_LANES:], o[r:, V7X_LANES:]))

    def fill_window(n, cls, n_cur):
        cs = slice(cls * DIL_WIDTH, (cls + 1) * DIL_WIDTH)
        for w_ref, (p_ref, c_ref, n_ref) in ((kw_ref, k_refs[n]), (vw_ref, v_refs[n])):
            w_ref[0:rad, :] = p_ref[0, :, cs]
            w_ref[rad:rad + n_cur, :] = c_ref[0, :, cs]
            w_ref[rad + n_cur:2 * rad + n_cur, :] = n_ref[0, :, cs]

    n_sub = DIL_SB // rows
    fill_window(0, 0, DIL_SB)

    def body(jj, carry):
        for u in range(DIL_UNROLL):
            j = jj * DIL_UNROLL + u
            r0 = pl.multiple_of(j * rows, rows)
            edge = (jnp.where(j == 0, no_prev, 0.0)
                    + jnp.where(j == n_sub - 1, no_next(win - rad), 0.0))
            for g in range(n_col):
                cs = slice(g * V7X_LANES, (g + 1) * V7X_LANES)
                acc, l = head_pair(q_refs[0][0, pl.ds(r0, rows), cs], kw_ref[pl.ds(r0, win), cs],
                                   vw_ref[pl.ds(r0, win), cs], tab_ref[0, g] + edge)
                acc_ref[g, pl.ds(r0, rows), :] = acc
                l_ref[g, pl.ds(r0, rows), :] = l
        return carry

    lax.fori_loop(0, n_sub // DIL_UNROLL, body, 0)

    for n, (_, dilation) in list(enumerate(DIL_PATTERNS))[1:]:
        n_cur = DIL_SB // dilation
        r = min(rows, n_cur)
        n_sub = n_cur // r
        for cls in range(dilation):
            fill_window(n, cls, n_cur)
            for j in range(n_sub):
                edge = ((no_prev if j == 0 else 0.0)
                        + (no_next(r + rad) if j == n_sub - 1 else 0.0))
                for g in range(n_col):
                    cs = slice(g * V7X_LANES, (g + 1) * V7X_LANES)
                    qs = slice(cls * DIL_WIDTH + g * V7X_LANES, cls * DIL_WIDTH + (g + 1) * V7X_LANES)
                    bias = jnp.concatenate([tab_ref[n, g, 0:r, :], tab_ref[n, g, rows:rows + r, :]],
                                           axis=0) + edge
                    acc, l = head_pair(q_refs[n][0, j * r:(j + 1) * r, qs],
                                       kw_ref[j * r:j * r + win, cs], vw_ref[j * r:j * r + win, cs],
                                       bias)
                    where = pl.ds(j * r * dilation + cls, r, stride=dilation)
                    acc_ref[g, where, :] += acc
                    l_ref[g, where, :] += l

    for g in range(n_col):
        o_ref[0, :, g * V7X_LANES:(g + 1) * V7X_LANES] = (acc_ref[g] / l_ref[g]).astype(o_ref.dtype)


def _dilated_fused(views, slopes_l2e):
    b = views[0][0].shape[0]
    s = views[0][0].shape[1]
    in_specs, args = [], []
    for (_, dilation), (bq, _, _) in zip(DIL_PATTERNS, views):
        in_specs.append(pl.BlockSpec((1, DIL_SB // dilation, dilation * DIL_WIDTH),
                                     lambda bi, i: (bi, i, 0)))
        args.append(bq)
    for tns in (1, 2):
        for (_, dilation), view in zip(DIL_PATTERNS, views):
            n_cur = DIL_SB // dilation
            per_block = n_cur // DIL_RADIUS
            n_halo = s // dilation // DIL_RADIUS
            in_specs += [
                pl.BlockSpec((1, DIL_RADIUS, dilation * DIL_WIDTH),
                             lambda bi, i, pb=per_block: (bi, jnp.maximum(i * pb - 1, 0), 0)),
                pl.BlockSpec((1, n_cur, dilation * DIL_WIDTH), lambda bi, i: (bi, i, 0)),
                pl.BlockSpec((1, DIL_RADIUS, dilation * DIL_WIDTH),
                             lambda bi, i, pb=per_block, nh=n_halo:
                             (bi, jnp.minimum((i + 1) * pb, nh - 1), 0)),
            ]
            args += [view[tns]] * 3
    n_col = DIL_WIDTH // V7X_LANES
    return pl.pallas_call(
        functools.partial(_dil_fused_kernel, slopes_l2e=slopes_l2e),
        grid=(b, s // DIL_SB),
        in_specs=in_specs,
        out_specs=pl.BlockSpec((1, DIL_SB, DIL_WIDTH), lambda bi, i: (bi, i, 0)),
        out_shape=jax.ShapeDtypeStruct((b, s, DIL_WIDTH), BF16),
        scratch_shapes=[
            pltpu.VMEM((len(DIL_PATTERNS), n_col, 2 * DIL_Q, DIL_Q + 2 * DIL_RADIUS), F32),
            pltpu.VMEM((DIL_SB + 2 * DIL_RADIUS, DIL_WIDTH), BF16),
            pltpu.VMEM((DIL_SB + 2 * DIL_RADIUS, DIL_WIDTH), BF16),
            pltpu.VMEM((n_col, DIL_SB, V7X_LANES), F32),
            pltpu.VMEM((n_col, DIL_SB, V7X_LANES), F32),
        ],
        compiler_params=pltpu.CompilerParams(
            dimension_semantics=("arbitrary", "arbitrary"), vmem_limit_bytes=DIL_VMEM_LIMIT),
        name="dil_fused",
    )(*args)


def _out_kernel(x_ref, p_ref, a_ref, sz_ref, b_ref, wout_ref, g2_ref, wpg_ref, wpp_ref, y_ref):
    sz = sz_ref[0].astype(F32)
    ya = (a_ref[0].astype(F32) * sz[:, :DIFF_WIDTH]).astype(BF16)
    yb = (b_ref[0].astype(F32) * sz[:, DIFF_WIDTH:]).astype(BF16)
    x1 = (x_ref[0]
          + jnp.dot(ya, wout_ref[:DIFF_WIDTH, :], preferred_element_type=F32)
          + jnp.dot(yb, wout_ref[DIFF_WIDTH:, :], preferred_element_type=F32))
    ms = jnp.mean(x1 * x1, axis=-1, keepdims=True)
    h2 = (x1 * lax.rsqrt(ms + NORM_EPS) * g2_ref[...]).astype(BF16)
    gate = jax.nn.sigmoid(jnp.dot(h2, wpg_ref[...], preferred_element_type=F32))
    pp = jnp.dot(p_ref[0].astype(BF16), wpp_ref[...], preferred_element_type=F32)
    y_ref[0] = x1 + gate * pp


def _out(x, p, a, sz, bmix, wout, g2, wpg, wpp):
    b, s, d = x.shape
    full = lambda shape: pl.BlockSpec(shape, lambda bi, i: (0,) * len(shape))
    row_spec = lambda w: pl.BlockSpec((1, OUT_ROWS, w), lambda bi, i: (bi, i, 0))
    return pl.pallas_call(
        _out_kernel,
        grid=(b, s // OUT_ROWS),
        in_specs=[row_spec(d), row_spec(p.shape[-1]), row_spec(DIFF_WIDTH), row_spec(MIX_WIDTH),
                  row_spec(DIL_WIDTH),
                  full(wout.shape), full((1, d)), full(wpg.shape), full(wpp.shape)],
        out_specs=row_spec(d),
        out_shape=jax.ShapeDtypeStruct((b, s, d), F32),
        compiler_params=pltpu.CompilerParams(
            dimension_semantics=("arbitrary", "arbitrary"), vmem_limit_bytes=VMEM_LIMIT),
        name="out_proj",
    )(x, p, a, sz, bmix, wout, g2, wpg, wpp)


def kernel(x, p, mix_norm_g, w_in, diff_q_norm_g, diff_k_norm_g, lambda_q1, lambda_k1,
           lambda_q2, lambda_k2, diff_sub_norm_g, dil_q_norm_g, dil_k_norm_g, w_out,
           ple_norm_g, w_ple_gate, w_ple_proj):
    depth = w_in.shape[0]
    diff_slopes = _alibi_slopes(DIFF_HEADS) * LOG2E
    dil_slopes = tuple(float(2.0 ** (-8.0 * (i + 1) / DIL_HEADS)) * LOG2E for i in range(DIL_HEADS))
    fold = QK_SCALE * LOG2E
    for i in range(depth):
        lam_init = 0.8 - 0.6 * math.exp(-0.3 * i)
        w = w_in[i]
        wqv_t = jnp.concatenate([w[:, :512], w[:, 1024:1536]], axis=1).T.astype(BF16)
        w_rest = jnp.concatenate([w[:, 512:1024], w[:, 1536:]], axis=1).astype(BF16)
        row2 = lambda g: jnp.tile(g, 2)[None, :]
        qt, vt, k, *dil_qkv, sz = _proj(
            x, mix_norm_g[i][None, :], wqv_t, w_rest,
            (diff_q_norm_g[i] * fold)[:, None], row2(diff_k_norm_g[i]),
            row2(dil_q_norm_g[i] * fold), row2(dil_k_norm_g[i]))
        views = tuple(tuple(dil_qkv[3 * n:3 * n + 3]) for n in range(len(DIL_PATTERNS)))
        lam_params = jnp.stack([lambda_q1[i], lambda_k1[i], lambda_q2[i], lambda_k2[i]])

        def score_bound(gq, gk):
            return 1.01 * HEAD_DIM * fold * jnp.max(jnp.abs(gq)) * jnp.max(jnp.abs(gk))

        a = _diff_attention(score_bound(diff_q_norm_g[i], diff_k_norm_g[i]).reshape(1), diff_slopes,
                            lam_params, diff_sub_norm_g[i][:, None], qt, k, vt, lam_init)
        bmix = lax.cond(score_bound(dil_q_norm_g[i], dil_k_norm_g[i]) <= FAST_MAX_SCORE,
                        lambda v: _dilated_fused(v, dil_slopes),
                        lambda v: _dilated_by_pattern(v, dil_slopes), views)
        x = _out(x, p[i], a, sz, bmix, w_out[i].astype(BF16), ple_norm_g[i][None, :],
                 w_ple_gate[i].astype(BF16), w_ple_proj[i].astype(BF16))
    return x
```

```python
import functools
import math

import jax
import jax.numpy as jnp
from jax import lax
from jax.experimental import pallas as pl
from jax.experimental.pallas import tpu as pltpu

F32 = jnp.float32
BF16 = jnp.bfloat16

HEAD_DIM = 64
DIFF_HEADS = 4
DIFF_V = 2 * HEAD_DIM
DIFF_WIDTH = DIFF_HEADS * DIFF_V
DIL_HEADS = 8
DIL_WIDTH = DIL_HEADS * HEAD_DIM
DIL_PATTERNS = ((128, 1), (512, 4), (2048, 16))
MIX_WIDTH = DIFF_WIDTH + DIL_WIDTH
NORM_EPS = 1e-6
MASK_VALUE = -1e30
LOG2E = math.log2(math.e)
QK_SCALE = HEAD_DIM ** -0.5

V7X_LANES = 128
PROJ_ROWS = 512
DIFF_TQ = 256
DIFF_SUBTILES = 2
DIFF_TK = 512
DIFF_FAST_TILES = 2
VT_ROWS = DIFF_V + 16
FAST_MAX_SCORE = 40.0
DIL_ROWS = 256
DIL_SB = 1024
DIL_Q = 128
DIL_UNROLL = 4
DIL_VMEM_LIMIT = 56 * 1024 * 1024
DIL_RADIUS = 64
OUT_ROWS = 512
VMEM_LIMIT = 48 * 1024 * 1024


def _alibi_slopes(n_heads):
    return jnp.exp2(-8.0 * jnp.arange(1, n_heads + 1, dtype=F32) / n_heads)


def _pair_head_norm(t, gain_row):
    lane = lax.broadcasted_iota(jnp.int32, t.shape, 1)
    lo = lane < HEAD_DIM
    sq = t * t
    ms_lo = jnp.sum(jnp.where(lo, sq, 0.0), axis=-1, keepdims=True) * (1.0 / HEAD_DIM)
    ms_hi = jnp.sum(jnp.where(lo, 0.0, sq), axis=-1, keepdims=True) * (1.0 / HEAD_DIM)
    r = jnp.where(lo, lax.rsqrt(ms_lo + NORM_EPS), lax.rsqrt(ms_hi + NORM_EPS))
    return t * r * gain_row


def _proj_kernel(x_ref, g_ref, wqv_t_ref, w_ref, gq_col_ref, gk_row_ref,
                 gbq_row_ref, gbk_row_ref,
                 qt_ref, vt_ref, k_ref, bq_ref, bk_ref, bv_ref,
                 bq4_ref, bk4_ref, bv4_ref, bq16_ref, bk16_ref, bv16_ref, sz_ref, ub_ref, ub4_ref):
    x = x_ref[0]
    ms = jnp.mean(x * x, axis=-1, keepdims=True)
    h = (x * lax.rsqrt(ms + NORM_EPS) * g_ref[...]).astype(BF16)


    ub = jnp.dot(h, w_ref[:, 512:2048], preferred_element_type=F32)
    n_col = DIL_WIDTH // V7X_LANES
    for pair in range(n_col):
        c0 = pair * V7X_LANES
        ub_ref[pair] = _pair_head_norm(ub[:, c0:c0 + V7X_LANES], gbq_row_ref[...])
        ub_ref[n_col + pair] = _pair_head_norm(ub[:, 512 + c0:512 + c0 + V7X_LANES],
                                               gbk_row_ref[...])
        ub_ref[2 * n_col + pair] = ub[:, 1024 + c0:1024 + c0 + V7X_LANES]
    nat_refs, d4_refs, d16_refs = ((bq_ref, bk_ref, bv_ref), (bq4_ref, bk4_ref, bv4_ref),
                                   (bq16_ref, bk16_ref, bv16_ref))
    n4 = PROJ_ROWS // 4
    for tns in range(3):
        for pair in range(n_col):
            slab = tns * n_col + pair
            c0 = pair * V7X_LANES
            nat_refs[tns][0, :, c0:c0 + V7X_LANES] = ub_ref[slab].astype(BF16)
            for c4 in range(4):
                rows4 = ub_ref[slab, pl.ds(c4, n4, stride=4), :]
                d4_refs[tns][0, :, c4 * DIL_WIDTH + c0:c4 * DIL_WIDTH + c0 + V7X_LANES] = (
                    rows4.astype(BF16))
                ub4_ref[slab, c4] = rows4
            for c4 in range(4):
                for e in range(4):
                    rows16 = ub4_ref[slab, c4, pl.ds(e, n4 // 4, stride=4), :]
                    cb = (4 * e + c4) * DIL_WIDTH + c0
                    d16_refs[tns][0, :, cb:cb + V7X_LANES] = rows16.astype(BF16)

    uk = jnp.dot(h, w_ref[:, 0:512], preferred_element_type=F32)
    for pair in range(DIFF_WIDTH // V7X_LANES):
        c0 = pair * V7X_LANES
        k_ref[0, :, c0:c0 + V7X_LANES] = _pair_head_norm(
            uk[:, c0:c0 + V7X_LANES], gk_row_ref[...]).astype(BF16)

    nt_dims = (((1,), (1,)), ((), ()))
    uqt = lax.dot_general(wqv_t_ref[0:DIFF_WIDTH, :], h, nt_dims, preferred_element_type=F32)
    for grp in range(DIFF_WIDTH // HEAD_DIM):
        t = uqt[grp * HEAD_DIM:(grp + 1) * HEAD_DIM, :]
        msq = jnp.mean(t * t, axis=0, keepdims=True)
        y = t * lax.rsqrt(msq + NORM_EPS) * gq_col_ref[...]
        qt_ref[0, grp * HEAD_DIM:(grp + 1) * HEAD_DIM, :] = y.astype(BF16)

    z = jnp.dot(h, w_ref[:, 2048:], preferred_element_type=F32)
    sz_ref[0] = (z * jax.nn.sigmoid(z)).astype(sz_ref.dtype)

    vt = lax.dot_general(wqv_t_ref[DIFF_WIDTH:, :], h, nt_dims,
                         preferred_element_type=F32).astype(BF16)
    ones = jnp.ones((VT_ROWS - DIFF_V, DIFF_TK), BF16)
    for j in range(PROJ_ROWS // DIFF_TK):
        for hd in range(DIFF_HEADS):
            vt_ref[0, j, hd * VT_ROWS:hd * VT_ROWS + DIFF_V, :] = (
                vt[hd * DIFF_V:(hd + 1) * DIFF_V, j * DIFF_TK:(j + 1) * DIFF_TK])
            vt_ref[0, j, hd * VT_ROWS + DIFF_V:(hd + 1) * VT_ROWS, :] = ones


def _proj(x, g, wqv_t, w_rest, gq_col, gk_row, gbq_row, gbk_row):
    b, s, d = x.shape
    n_row = s // PROJ_ROWS
    full = lambda shape: pl.BlockSpec(shape, lambda bi, i: (0,) * len(shape))
    out_shape = (
        jax.ShapeDtypeStruct((b, DIFF_WIDTH, s), BF16),
        jax.ShapeDtypeStruct((b, s // DIFF_TK, DIFF_HEADS * VT_ROWS, DIFF_TK), BF16),
        jax.ShapeDtypeStruct((b, s, DIFF_WIDTH), BF16),
    ) + tuple(
        jax.ShapeDtypeStruct((b, s // dil, dil * DIL_WIDTH), BF16)
        for _, dil in DIL_PATTERNS for _ in range(3)
    ) + (jax.ShapeDtypeStruct((b, s, MIX_WIDTH), BF16),)
    row_spec = lambda w: pl.BlockSpec((1, PROJ_ROWS, w), lambda bi, i: (bi, i, 0))
    view_spec = lambda dil: pl.BlockSpec((1, PROJ_ROWS // dil, dil * DIL_WIDTH),
                                         lambda bi, i: (bi, i, 0))
    return pl.pallas_call(
        _proj_kernel,
        grid=(b, n_row),
        in_specs=[
            pl.BlockSpec((1, PROJ_ROWS, d), lambda bi, i: (bi, i, 0)),
            full((1, d)), full(wqv_t.shape), full(w_rest.shape),
            full((HEAD_DIM, 1)), full((1, V7X_LANES)), full((1, V7X_LANES)), full((1, V7X_LANES)),
        ],
        out_specs=(
            pl.BlockSpec((1, DIFF_WIDTH, PROJ_ROWS), lambda bi, i: (bi, 0, i)),
            pl.BlockSpec((1, PROJ_ROWS // DIFF_TK, DIFF_HEADS * VT_ROWS, DIFF_TK),
                         lambda bi, i: (bi, i, 0, 0)),
            row_spec(DIFF_WIDTH),
        ) + tuple(view_spec(dil) for _, dil in DIL_PATTERNS for _ in range(3)) + (
            row_spec(MIX_WIDTH),
        ),
        out_shape=out_shape,
        scratch_shapes=[pltpu.VMEM((3 * DIL_WIDTH // V7X_LANES, PROJ_ROWS, V7X_LANES), F32),
                        pltpu.VMEM((3 * DIL_WIDTH // V7X_LANES, 4, PROJ_ROWS // 4, V7X_LANES), F32)],
        compiler_params=pltpu.CompilerParams(
            dimension_semantics=("arbitrary", "arbitrary"), vmem_limit_bytes=VMEM_LIMIT),
        name="proj",
    )(x, g, wqv_t, w_rest, gq_col, gk_row, gbq_row, gbk_row)


def _diff_kernel(bound_ref, slope_ref, lam_ref, gsub_col_ref, qt_ref, k_ref, vt_ref,
                 o_ref, acc_ref, *, lam_init, n_kv):
    head = pl.program_id(1)
    iq = pl.program_id(2)
    slope = slope_ref[head]
    shift_free = bound_ref[0] <= FAST_MAX_SCORE
    lam_p = lam_ref[...]
    lam = (jnp.exp(jnp.sum(lam_p[0:1] * lam_p[1:2], axis=-1, keepdims=True))
           - jnp.exp(jnp.sum(lam_p[2:3] * lam_p[3:4], axis=-1, keepdims=True)) + lam_init)

    def query_weights(sub):
        qt = qt_ref[0, :, sub * DIFF_TQ:(sub + 1) * DIFF_TQ]
        row = lax.broadcasted_iota(jnp.int32, qt.shape, 0)
        zero = jnp.zeros_like(qt)
        return jnp.where(row < HEAD_DIM, qt, zero), jnp.where(row < HEAD_DIM, zero, qt)

    def finish(sub, num, den):
        a = num[0] / den[0] - lam * (num[1] / den[1])
        ms = jnp.mean(a * a, axis=0, keepdims=True)
        y = a * lax.rsqrt(ms + NORM_EPS) * gsub_col_ref[...] * (1.0 - lam_init)
        o_ref[0, sub * DIFF_TQ:(sub + 1) * DIFF_TQ, :] = y.T.astype(o_ref.dtype)

    @pl.when(shift_free)
    def _():
        tk = DIFF_FAST_TILES * DIFF_TK
        n_t = n_kv // DIFF_FAST_TILES
        rel = (lax.broadcasted_iota(jnp.int32, (tk, DIFF_TQ), 0)
               - lax.broadcasted_iota(jnp.int32, (tk, DIFF_TQ), 1)).astype(F32)
        for sub in range(DIFF_SUBTILES):
            w_half = query_weights(sub)
            i0 = (iq * DIFF_SUBTILES + sub) * DIFF_TQ

            def weights(t):
                kt = k_ref[0, t * tk:(t + 1) * tk, :]
                bias = slope * jnp.abs(rel + (t * tk - i0).astype(F32))
                return [jnp.exp2(jnp.dot(kt, w_half[c], preferred_element_type=F32) - bias
                                 ).astype(BF16) for c in range(2)]

            acc = [jnp.zeros((VT_ROWS, DIFF_TQ), F32)] * 2
            p_cur = weights(0)
            for t in range(n_t):
                p_next = weights(t + 1) if t + 1 < n_t else None
                vt = jnp.concatenate([vt_ref[0, t * DIFF_FAST_TILES + ch]
                                      for ch in range(DIFF_FAST_TILES)], axis=1)
                acc = [acc[c] + jnp.dot(vt, p_cur[c], preferred_element_type=F32)
                       for c in range(2)]
                p_cur = p_next
            finish(sub, [a[0:DIFF_V] for a in acc], [a[DIFF_V:DIFF_V + 1] for a in acc])

    @pl.when(jnp.logical_not(shift_free))
    def _():
        for sub in range(DIFF_SUBTILES):
            w_half = query_weights(sub)
            i0 = (iq * DIFF_SUBTILES + sub) * DIFF_TQ
            acc_ref[...] = jnp.zeros_like(acc_ref)

            def body(kv, carry):
                j0 = pl.multiple_of(kv * DIFF_TK, DIFF_TK)
                kt = k_ref[0, pl.ds(j0, DIFF_TK), :]
                vt = vt_ref[0, kv]
                rel = (lax.broadcasted_iota(jnp.int32, (DIFF_TK, DIFF_TQ), 0)
                       - lax.broadcasted_iota(jnp.int32, (DIFF_TK, DIFF_TQ), 1) + (j0 - i0))
                bias = slope * jnp.abs(rel).astype(F32)
                new = []
                for c in range(2):
                    s = jnp.dot(kt, w_half[c], preferred_element_type=F32) - bias
                    m_new = jnp.maximum(carry[c], jnp.max(s, axis=0, keepdims=True))
                    alpha = jnp.exp2(carry[c] - m_new)
                    p = jnp.exp2(s - m_new)
                    acc_ref[c] = alpha * acc_ref[c] + jnp.dot(vt, p.astype(BF16),
                                                              preferred_element_type=F32)
                    new.append(m_new)
                return tuple(new)

            init = jnp.full((1, DIFF_TQ), MASK_VALUE, F32)
            lax.fori_loop(0, n_kv, body, (init, init))
            finish(sub, [acc_ref[c, 0:DIFF_V, :] for c in range(2)],
                   [acc_ref[c, DIFF_V:DIFF_V + 1, :] for c in range(2)])


def _diff_attention(score_bound, slopes_l2e, lam_params, gsub_col, qt, k, vt, lam_init):
    b, s, _ = k.shape
    n_kv = s // DIFF_TK
    step = DIFF_SUBTILES * DIFF_TQ
    kern = functools.partial(_diff_kernel, lam_init=lam_init, n_kv=n_kv)
    return pl.pallas_call(
        kern,
        grid=(b, DIFF_HEADS, s // step),
        in_specs=[
            pl.BlockSpec(memory_space=pltpu.SMEM),
            pl.BlockSpec(memory_space=pltpu.SMEM),
            pl.BlockSpec((4, HEAD_DIM), lambda bi, h, i: (0, 0)),
            pl.BlockSpec((DIFF_V, 1), lambda bi, h, i: (0, 0)),
            pl.BlockSpec((1, DIFF_V, step), lambda bi, h, i: (bi, h, i)),
            pl.BlockSpec((1, s, DIFF_V), lambda bi, h, i: (bi, 0, h)),
            pl.BlockSpec((1, n_kv, VT_ROWS, DIFF_TK), lambda bi, h, i: (bi, 0, h, 0)),
        ],
        out_specs=pl.BlockSpec((1, step, DIFF_V), lambda bi, h, i: (bi, i, h)),
        out_shape=jax.ShapeDtypeStruct((b, s, DIFF_WIDTH), BF16),
        scratch_shapes=[pltpu.VMEM((2, VT_ROWS, DIFF_TQ), F32)],
        compiler_params=pltpu.CompilerParams(
            dimension_semantics=("arbitrary", "arbitrary", "arbitrary"),
            vmem_limit_bytes=VMEM_LIMIT),
        name="diff_attn",
    )(score_bound, slopes_l2e, lam_params, gsub_col, qt, k, vt)


def _dil_kernel(q_ref, kp_ref, kc_ref, kn_ref, vp_ref, vc_ref, vn_ref, o_ref, lse_ref,
                *, dilation, n_class_rows, slopes_l2e):
    i = pl.program_id(2)
    rows = DIL_ROWS
    win = rows + 2 * DIL_RADIUS
    kw = jnp.concatenate([kp_ref[0], kc_ref[0], kn_ref[0]], axis=0)
    vw = jnp.concatenate([vp_ref[0], vc_ref[0], vn_ref[0]], axis=0)
    r_idx = lax.broadcasted_iota(jnp.int32, (rows, win), 0)
    w_idx = lax.broadcasted_iota(jnp.int32, (rows, win), 1)
    rel = jnp.abs(r_idx - (w_idx - DIL_RADIUS))
    kj = i * rows - DIL_RADIUS + w_idx
    valid = (rel <= DIL_RADIUS) & (kj >= 0) & (kj < n_class_rows)
    dist = (rel * dilation).astype(F32)
    lane = lax.broadcasted_iota(jnp.int32, (rows, V7X_LANES), 1)
    lo = lane < HEAD_DIM
    for pair in range(DIL_WIDTH // V7X_LANES):
        c0 = pair * V7X_LANES
        qp = q_ref[0, :, c0:c0 + V7X_LANES]
        kp = kw[:, c0:c0 + V7X_LANES]
        vp = vw[:, c0:c0 + V7X_LANES]
        halves = []
        for e in range(2):
            qm = jnp.where(lo if e == 0 else jnp.logical_not(lo), qp, jnp.zeros_like(qp))
            s = lax.dot_general(qm, kp, (((1,), (1,)), ((), ())),
                                preferred_element_type=F32)
            s = jnp.where(valid, s - slopes_l2e[2 * pair + e] * dist, MASK_VALUE)
            m = jnp.max(s, axis=-1, keepdims=True)
            p = jnp.exp2(s - m)
            l = jnp.sum(p, axis=-1, keepdims=True)
            o = jnp.dot(p.astype(BF16), vp, preferred_element_type=F32)
            halves.append((o / l, m + jnp.log2(l)))
        o_ref[0, :, c0:c0 + V7X_LANES] = jnp.where(lo, halves[0][0], halves[1][0])
        lse_ref[0, :, c0:c0 + V7X_LANES] = jnp.where(lo, halves[0][1], halves[1][1])


def _dilated_pattern(bq, bk, bv, dilation, slopes_l2e):
    b, n, _ = bq.shape
    w = DIL_WIDTH
    s = n * dilation
    halo_per_block = DIL_ROWS // DIL_RADIUS
    n_halo = n // DIL_RADIUS
    cur = pl.BlockSpec((1, DIL_ROWS, w), lambda bi, c, i: (bi, i, c))
    prev = pl.BlockSpec((1, DIL_RADIUS, w),
                        lambda bi, c, i: (bi, jnp.maximum(i * halo_per_block - 1, 0), c))
    nxt = pl.BlockSpec((1, DIL_RADIUS, w),
                       lambda bi, c, i: (bi, jnp.minimum((i + 1) * halo_per_block, n_halo - 1), c))
    kern = functools.partial(_dil_kernel, dilation=dilation, n_class_rows=n,
                             slopes_l2e=slopes_l2e)
    o, lse = pl.pallas_call(
        kern,
        grid=(b, dilation, n // DIL_ROWS),
        in_specs=[cur, prev, cur, nxt, prev, cur, nxt],
        out_specs=(cur, cur),
        out_shape=(jax.ShapeDtypeStruct((b, n, dilation * w), F32),
                   jax.ShapeDtypeStruct((b, n, dilation * w), F32)),
        compiler_params=pltpu.CompilerParams(
            dimension_semantics=("arbitrary", "arbitrary", "arbitrary"),
            vmem_limit_bytes=VMEM_LIMIT),
        name=f"dil_attn_d{dilation}",
    )(bq, bk, bk, bk, bv, bv, bv)
    return o.reshape(b, s, w), lse.reshape(b, s, w)


def _dil_merge_kernel(o1_ref, o2_ref, o3_ref, l1_ref, l2_ref, l3_ref, b_ref):
    l1, l2, l3 = l1_ref[0], l2_ref[0], l3_ref[0]
    m = jnp.maximum(jnp.maximum(l1, l2), l3)
    w1, w2, w3 = jnp.exp2(l1 - m), jnp.exp2(l2 - m), jnp.exp2(l3 - m)
    b_ref[0] = ((w1 * o1_ref[0] + w2 * o2_ref[0] + w3 * o3_ref[0])
                / (w1 + w2 + w3)).astype(b_ref.dtype)


def _dilated_by_pattern(views, slopes_l2e):
    outs, lses = [], []
    for (_, dilation), (bq, bk, bv) in zip(DIL_PATTERNS, views):
        o, lse = _dilated_pattern(bq, bk, bv, dilation, slopes_l2e)
        outs.append(o)
        lses.append(lse)
    b, s, w = outs[0].shape
    spec = pl.BlockSpec((1, OUT_ROWS, w), lambda bi, i: (bi, i, 0))
    return pl.pallas_call(
        _dil_merge_kernel,
        grid=(b, s // OUT_ROWS),
        in_specs=[spec] * 6,
        out_specs=spec,
        out_shape=jax.ShapeDtypeStruct((b, s, w), BF16),
        compiler_params=pltpu.CompilerParams(
            dimension_semantics=("arbitrary", "arbitrary"), vmem_limit_bytes=VMEM_LIMIT),
        name="dil_merge",
    )(*outs, *lses)


def _dil_fused_kernel(*refs, slopes_l2e):
    q_refs = refs[0:3]
    k_refs = [refs[3 + 3 * n:6 + 3 * n] for n in range(3)]
    v_refs = [refs[12 + 3 * n:15 + 3 * n] for n in range(3)]
    o_ref, tab_ref, kw_ref, vw_ref, acc_ref, l_ref = refs[21:]
    bi, i = pl.program_id(0), pl.program_id(1)
    rows, win, rad = DIL_Q, DIL_Q + 2 * DIL_RADIUS, DIL_RADIUS
    n_col = DIL_WIDTH // V7X_LANES

    @pl.when(jnp.logical_and(bi == 0, i == 0))
    def _():
        rel = jnp.abs(lax.broadcasted_iota(jnp.int32, (rows, win), 0)
                      - lax.broadcasted_iota(jnp.int32, (rows, win), 1) + rad)
        for n, (_, dilation) in enumerate(DIL_PATTERNS):
            dist = (rel * dilation).astype(F32)
            for hd in range(DIL_HEADS):
                tab_ref[n, hd // 2, (hd % 2) * rows:(hd % 2 + 1) * rows, :] = jnp.where(
                    rel <= rad, -slopes_l2e[hd] * dist, MASK_VALUE)

    col = lax.broadcasted_iota(jnp.int32, (1, win), 1)
    no_prev = jnp.where(jnp.logical_and(col < rad, i == 0), MASK_VALUE, 0.0)
    ones = jnp.ones((win, V7X_LANES), BF16)

    def no_next(start):
        return jnp.where(jnp.logical_and(col >= start, i == pl.num_programs(1) - 1),
                         MASK_VALUE, 0.0)

    def head_pair(qp, kwin, vwin, bias):
        r = qp.shape[0]
        lo = lax.broadcasted_iota(jnp.int32, (r, V7X_LANES), 1) < HEAD_DIM
        zq = jnp.zeros_like(qp)
        lhs = jnp.concatenate([jnp.where(lo, qp, zq), jnp.where(lo, zq, qp)], axis=0)
        s = lax.dot_general(lhs, kwin, (((1,), (1,)), ((), ())),
                            preferred_element_type=F32) + bias
        o = jnp.dot(jnp.exp2(s).astype(BF16), jnp.concatenate([vwin, ones], axis=1),
                    preferred_element_type=F32)
        return (jnp.where(lo, o[:r, :V7X_LANES], o[r:, :V7X_LANES]),
                jnp.where(lo, o[:r, V7X_LANES:], o[r:, V7X_LANES:]))

    def fill_window(n, cls, n_cur):
        cs = slice(cls * DIL_WIDTH, (cls + 1) * DIL_WIDTH)
        for w_ref, (p_ref, c_ref, n_ref) in ((kw_ref, k_refs[n]), (vw_ref, v_refs[n])):
            w_ref[0:rad, :] = p_ref[0, :, cs]
            w_ref[rad:rad + n_cur, :] = c_ref[0, :, cs]
            w_ref[rad + n_cur:2 * rad + n_cur, :] = n_ref[0, :, cs]

    n_sub = DIL_SB // rows
    fill_window(0, 0, DIL_SB)

    def body(jj, carry):
        for u in range(DIL_UNROLL):
            j = jj * DIL_UNROLL + u
            r0 = pl.multiple_of(j * rows, rows)
            edge = (jnp.where(j == 0, no_prev, 0.0)
                    + jnp.where(j == n_sub - 1, no_next(win - rad), 0.0))
            for g in range(n_col):
                cs = slice(g * V7X_LANES, (g + 1) * V7X_LANES)
                acc, l = head_pair(q_refs[0][0, pl.ds(r0, rows), cs], kw_ref[pl.ds(r0, win), cs],
                                   vw_ref[pl.ds(r0, win), cs], tab_ref[0, g] + edge)
                acc_ref[g, pl.ds(r0, rows), :] = acc
                l_ref[g, pl.ds(r0, rows), :] = l
        return carry

    lax.fori_loop(0, n_sub // DIL_UNROLL, body, 0)

    for n, (_, dilation) in list(enumerate(DIL_PATTERNS))[1:]:
        n_cur = DIL_SB // dilation
        r = min(rows, n_cur)
        n_sub = n_cur // r
        for cls in range(dilation):
            fill_window(n, cls, n_cur)
            for j in range(n_sub):
                edge = ((no_prev if j == 0 else 0.0)
                        + (no_next(r + rad) if j == n_sub - 1 else 0.0))
                for g in range(n_col):
                    cs = slice(g * V7X_LANES, (g + 1) * V7X_LANES)
                    qs = slice(cls * DIL_WIDTH + g * V7X_LANES, cls * DIL_WIDTH + (g + 1) * V7X_LANES)
                    bias = jnp.concatenate([tab_ref[n, g, 0:r, :], tab_ref[n, g, rows:rows + r, :]],
                                           axis=0) + edge
                    acc, l = head_pair(q_refs[n][0, j * r:(j + 1) * r, qs],
                                       kw_ref[j * r:j * r + win, cs], vw_ref[j * r:j * r + win, cs],
                                       bias)
                    where = pl.ds(j * r * dilation + cls, r, stride=dilation)
                    acc_ref[g, where, :] += acc
                    l_ref[g, where, :] += l

    for g in range(n_col):
        o_ref[0, :, g * V7X_LANES:(g + 1) * V7X_LANES] = (acc_ref[g] / l_ref[g]).astype(o_ref.dtype)


def _dilated_fused(views, slopes_l2e):
    b = views[0][0].shape[0]
    s = views[0][0].shape[1]
    in_specs, args = [], []
    for (_, dilation), (bq, _, _) in zip(DIL_PATTERNS, views):
        in_specs.append(pl.BlockSpec((1, DIL_SB // dilation, dilation * DIL_WIDTH),
                                     lambda bi, i: (bi, i, 0)))
        args.append(bq)
    for tns in (1, 2):
        for (_, dilation), view in zip(DIL_PATTERNS, views):
            n_cur = DIL_SB // dilation
            per_block = n_cur // DIL_RADIUS
            n_halo = s // dilation // DIL_RADIUS
            in_specs += [
                pl.BlockSpec((1, DIL_RADIUS, dilation * DIL_WIDTH),
                             lambda bi, i, pb=per_block: (bi, jnp.maximum(i * pb - 1, 0), 0)),
                pl.BlockSpec((1, n_cur, dilation * DIL_WIDTH), lambda bi, i: (bi, i, 0)),
                pl.BlockSpec((1, DIL_RADIUS, dilation * DIL_WIDTH),
                             lambda bi, i, pb=per_block, nh=n_halo:
                             (bi, jnp.minimum((i + 1) * pb, nh - 1), 0)),
            ]
            args += [view[tns]] * 3
    n_col = DIL_WIDTH // V7X_LANES
    return pl.pallas_call(
        functools.partial(_dil_fused_kernel, slopes_l2e=slopes_l2e),
        grid=(b, s // DIL_SB),
        in_specs=in_specs,
        out_specs=pl.BlockSpec((1, DIL_SB, DIL_WIDTH), lambda bi, i: (bi, i, 0)),
        out_shape=jax.ShapeDtypeStruct((b, s, DIL_WIDTH), BF16),
        scratch_shapes=[
            pltpu.VMEM((len(DIL_PATTERNS), n_col, 2 * DIL_Q, DIL_Q + 2 * DIL_RADIUS), F32),
            pltpu.VMEM((DIL_SB + 2 * DIL_RADIUS, DIL_WIDTH), BF16),
            pltpu.VMEM((DIL_SB + 2 * DIL_RADIUS, DIL_WIDTH), BF16),
            pltpu.VMEM((n_col, DIL_SB, V7X_LANES), F32),
            pltpu.VMEM((n_col, DIL_SB, V7X_LANES), F32),
        ],
        compiler_params=pltpu.CompilerParams(
            dimension_semantics=("arbitrary", "arbitrary"), vmem_limit_bytes=DIL_VMEM_LIMIT),
        name="dil_fused",
    )(*args)


def _out_kernel(x_ref, p_ref, a_ref, sz_ref, b_ref, wout_ref, g2_ref, wpg_ref, wpp_ref, y_ref):
    sz = sz_ref[0].astype(F32)
    ya = (a_ref[0].astype(F32) * sz[:, :DIFF_WIDTH]).astype(BF16)
    yb = (b_ref[0].astype(F32) * sz[:, DIFF_WIDTH:]).astype(BF16)
    x1 = (x_ref[0]
          + jnp.dot(ya, wout_ref[:DIFF_WIDTH, :], preferred_element_type=F32)
          + jnp.dot(yb, wout_ref[DIFF_WIDTH:, :], preferred_element_type=F32))
    ms = jnp.mean(x1 * x1, axis=-1, keepdims=True)
    h2 = (x1 * lax.rsqrt(ms + NORM_EPS) * g2_ref[...]).astype(BF16)
    gate = jax.nn.sigmoid(jnp.dot(h2, wpg_ref[...], preferred_element_type=F32))
    pp = jnp.dot(p_ref[0].astype(BF16), wpp_ref[...], preferred_element_type=F32)
    y_ref[0] = x1 + gate * pp


def _out(x, p, a, sz, bmix, wout, g2, wpg, wpp):
    b, s, d = x.shape
    full = lambda shape: pl.BlockSpec(shape, lambda bi, i: (0,) * len(shape))
    row_spec = lambda w: pl.BlockSpec((1, OUT_ROWS, w), lambda bi, i: (bi, i, 0))
    return pl.pallas_call(
        _out_kernel,
        grid=(b, s // OUT_ROWS),
        in_specs=[row_spec(d), row_spec(p.shape[-1]), row_spec(DIFF_WIDTH), row_spec(MIX_WIDTH),
                  row_spec(DIL_WIDTH),
                  full(wout.shape), full((1, d)), full(wpg.shape), full(wpp.shape)],
        out_specs=row_spec(d),
        out_shape=jax.ShapeDtypeStruct((b, s, d), F32),
        compiler_params=pltpu.CompilerParams(
            dimension_semantics=("arbitrary", "arbitrary"), vmem_limit_bytes=VMEM_LIMIT),
        name="out_proj",
    )(x, p, a, sz, bmix, wout, g2, wpg, wpp)


def kernel(x, p, mix_norm_g, w_in, diff_q_norm_g, diff_k_norm_g, lambda_q1, lambda_k1,
           lambda_q2, lambda_k2, diff_sub_norm_g, dil_q_norm_g, dil_k_norm_g, w_out,
           ple_norm_g, w_ple_gate, w_ple_proj):
    depth = w_in.shape[0]
    diff_slopes = _alibi_slopes(DIFF_HEADS) * LOG2E
    dil_slopes = tuple(float(2.0 ** (-8.0 * (i + 1) / DIL_HEADS)) * LOG2E for i in range(DIL_HEADS))
    fold = QK_SCALE * LOG2E
    for i in range(depth):
        lam_init = 0.8 - 0.6 * math.exp(-0.3 * i)
        w = w_in[i]
        wqv_t = jnp.concatenate([w[:, :512], w[:, 1024:1536]], axis=1).T.astype(BF16)
        w_rest = jnp.concatenate([w[:, 512:1024], w[:, 1536:]], axis=1).astype(BF16)
        row2 = lambda g: jnp.tile(g, 2)[None, :]
        qt, vt, k, *dil_qkv, sz = _proj(
            x, mix_norm_g[i][None, :], wqv_t, w_rest,
            (diff_q_norm_g[i] * fold)[:, None], row2(diff_k_norm_g[i]),
            row2(dil_q_norm_g[i] * fold), row2(dil_k_norm_g[i]))
        views = tuple(tuple(dil_qkv[3 * n:3 * n + 3]) for n in range(len(DIL_PATTERNS)))
        lam_params = jnp.stack([lambda_q1[i], lambda_k1[i], lambda_q2[i], lambda_k2[i]])

        def score_bound(gq, gk):
            return 1.01 * HEAD_DIM * fold * jnp.max(jnp.abs(gq)) * jnp.max(jnp.abs(gk))

        a = _diff_attention(score_bound(diff_q_norm_g[i], diff_k_norm_g[i]).reshape(1), diff_slopes,
                            lam_params, diff_sub_norm_g[i][:, None], qt, k, vt, lam_init)
        bmix = lax.cond(score_bound(dil_q_norm_g[i], dil_k_norm_g[i]) <= FAST_MAX_SCORE,
                        lambda v: _dilated_fused(v, dil_slopes),
                        lambda v: _dilated_by_pattern(v, dil_slopes), views)
        x = _out(x, p[i], a, sz, bmix, w_out[i].astype(BF16), ple_norm_g[i][None, :],
                 w_ple_gate[i].astype(BF16), w_ple_proj[i].astype(BF16))
    return x
```
